```python
import jax
import jax.numpy as jnp
from jax import lax
import numpy as np

D_MODEL = 2048
BATCH = 8
SEQ = 2048
DEPTH = 1

CHUNK = 64
LEFT_CHUNKS = 8
BAND = (LEFT_CHUNKS + 1) * CHUNK
HEAD_DIM = 128
ATTN_WIDTH = D_MODEL // 2
ATTN_HEADS = ATTN_WIDTH // HEAD_DIM
MAX_REL = 128
GMLP_WIDTH = D_MODEL - ATTN_WIDTH
GMLP_GROUP_DIM = 128
GMLP_GROUPS = GMLP_WIDTH // GMLP_GROUP_DIM
SPATIAL = 128
MIX_WIDTH = ATTN_WIDTH + GMLP_WIDTH
IN_WIDTH = 3 * ATTN_WIDTH + 2 * GMLP_WIDTH
D_FF = 4 * D_MODEL
N_MOD = 6
EPS = 1e-6
NEG_INF = -1e30

kernel_name = 'streaming_hybrid_attn_gmlp_block'


def rms_norm(x, g):
    xf = x.astype(jnp.float32)
    y = xf * lax.rsqrt(jnp.mean(xf * xf, axis=-1, keepdims=True) + EPS)
    return (y * g.astype(jnp.float32)).astype(x.dtype)


def modulate(h, shift, scale):
    return h * (1 + scale[:, None, :]) + shift[:, None, :]


def chunk_band(t, nc):
    tp = jnp.pad(t, ((0, 0), (LEFT_CHUNKS, 0), (0, 0), (0, 0), (0, 0)))
    return jnp.concatenate([tp[:, i:i + nc] for i in range(LEFT_CHUNKS + 1)], axis=2)


def chunked_rel_attention(q, k, v, q_g, k_g, rel_bias):
    B, S = q.shape[0], q.shape[1]
    nc = S // CHUNK
    q = rms_norm(q, q_g).reshape(B, nc, CHUNK, ATTN_HEADS, HEAD_DIM)
    k = chunk_band(rms_norm(k, k_g).reshape(B, nc, CHUNK, ATTN_HEADS, HEAD_DIM), nc)
    v = chunk_band(v.reshape(B, nc, CHUNK, ATTN_HEADS, HEAD_DIM), nc)
    scores = jnp.einsum('bcqhd,bckhd->bhcqk', q, k).astype(jnp.float32) * (HEAD_DIM ** -0.5)
    qi = jnp.arange(CHUNK)[:, None]
    kj = jnp.arange(BAND)[None, :]
    rel = jnp.clip(qi + LEFT_CHUNKS * CHUNK - kj, -MAX_REL, MAX_REL) + MAX_REL
    bias = rel_bias.astype(jnp.float32)[:, rel]
    key_chunk = jnp.arange(nc)[:, None] - LEFT_CHUNKS + jnp.arange(BAND)[None, :] // CHUNK
    valid = key_chunk >= 0
    scores = jnp.where(valid[None, None, :, None, :], scores + bias[None, :, None], NEG_INF)
    probs = jax.nn.softmax(scores, axis=-1).astype(v.dtype)
    out = jnp.einsum('bhcqk,bckhd->bcqhd', probs, v)
    return out.reshape(B, S, ATTN_WIDTH)


def spatial_gating(u, v, v_g, w_s, b_s):
    B, S = u.shape[0], u.shape[1]
    nb = S // SPATIAL
    v = rms_norm(v.reshape(B, S, GMLP_GROUPS, GMLP_GROUP_DIM), v_g)
    v = v.reshape(B, nb, SPATIAL, GMLP_GROUPS, GMLP_GROUP_DIM)
    t = jnp.arange(SPATIAL)
    mask = (t[:, None] // CHUNK) >= (t[None, :] // CHUNK)
    w = jnp.where(mask[None], w_s, 0)
    y = jnp.einsum('gts,bnsgc->bntgc', w, v) + jnp.transpose(b_s)[:, :, None]
    out = u.reshape(B, nb, SPATIAL, GMLP_GROUPS, GMLP_GROUP_DIM) * y
    return out.reshape(B, S, GMLP_WIDTH)


def setup_inputs(seed: int = 0) -> dict:
    key = jax.random.key(seed)
    ks = jax.random.split(key, 18)
    f32 = jnp.float32
    L = DEPTH

    def nrm(k, shape, s):
        return jax.random.normal(k, shape, f32) * s

    return {
        'x': nrm(ks[0], (BATCH, SEQ, D_MODEL), 1.0),
        'c': nrm(ks[1], (BATCH, D_MODEL), 1.0),
        'w_ada': nrm(ks[2], (L, D_MODEL, N_MOD * D_MODEL), D_MODEL ** -0.5),
        'b_ada': nrm(ks[3], (L, N_MOD * D_MODEL), 0.01),
        'mix_norm_g': 1.0 + nrm(ks[4], (L, D_MODEL), 0.01),
        'w_in': nrm(ks[5], (L, D_MODEL, IN_WIDTH), D_MODEL ** -0.5),
        'q_norm_g': 1.0 + nrm(ks[6], (L, HEAD_DIM), 0.01),
        'k_norm_g': 1.0 + nrm(ks[7], (L, HEAD_DIM), 0.01),
        'rel_bias': nrm(ks[8], (L, ATTN_HEADS, 2 * MAX_REL + 1), 0.1),
        'gmlp_norm_g': 1.0 + nrm(ks[9], (L, GMLP_GROUPS, GMLP_GROUP_DIM), 0.01),
        'w_spatial': nrm(ks[10], (L, GMLP_GROUPS, SPATIAL, SPATIAL), SPATIAL ** -0.5),
        'b_spatial': 1.0 + nrm(ks[11], (L, GMLP_GROUPS, SPATIAL), 0.01),
        'attn_out_g': 1.0 + nrm(ks[12], (L, ATTN_WIDTH), 0.01),
        'gmlp_out_g': 1.0 + nrm(ks[13], (L, GMLP_WIDTH), 0.01),
        'w_out': nrm(ks[14], (L, MIX_WIDTH, D_MODEL), MIX_WIDTH ** -0.5),
        'ff_norm_g': 1.0 + nrm(ks[15], (L, D_MODEL), 0.01),
        'w_ff1': nrm(ks[16], (L, D_MODEL, D_FF), D_MODEL ** -0.5),
        'w_ff2': nrm(ks[17], (L, D_FF, D_MODEL), D_FF ** -0.5),
    }


def reference(x, c, w_ada, b_ada, mix_norm_g, w_in, q_norm_g, k_norm_g, rel_bias, gmlp_norm_g,
              w_spatial, b_spatial, attn_out_g, gmlp_out_g, w_out, ff_norm_g, w_ff1, w_ff2):
    B, S = x.shape[0], x.shape[1]
    head_shape = (B, S, ATTN_HEADS, HEAD_DIM)
    cond = jax.nn.silu(c)
    for l in range(DEPTH):
        mod = cond @ w_ada[l] + b_ada[l]
        shift_m, scale_m, gate_m, shift_f, scale_f, gate_f = jnp.split(mod, N_MOD, axis=-1)
        h = modulate(rms_norm(x, mix_norm_g[l]), shift_m, scale_m)
        proj = h @ w_in[l]
        q, k, v, u, z = jnp.split(
            proj, [ATTN_WIDTH, 2 * ATTN_WIDTH, 3 * ATTN_WIDTH, 3 * ATTN_WIDTH + GMLP_WIDTH], axis=-1)
        attn = chunked_rel_attention(q.reshape(head_shape), k.reshape(head_shape), v.reshape(head_shape),
                                     q_norm_g[l], k_norm_g[l], rel_bias[l])
        gm = spatial_gating(jax.nn.gelu(u, approximate=False), jax.nn.gelu(z, approximate=False),
                            gmlp_norm_g[l], w_spatial[l], b_spatial[l])
        mix = jnp.concatenate([rms_norm(attn, attn_out_g[l]), rms_norm(gm, gmlp_out_g[l])], axis=-1)
        x = x + gate_m[:, None, :] * (mix @ w_out[l])
        h = modulate(rms_norm(x, ff_norm_g[l]), shift_f, scale_f)
        x = x + gate_f[:, None, :] * (jnp.square(jax.nn.relu(h @ w_ff1[l])) @ w_ff2[l])
    return x
```

```python
import functools

import jax
import jax.numpy as jnp
from jax import lax
from jax.experimental import pallas as pl
from jax.experimental.pallas import tpu as pltpu

CHUNK = 64
LEFT_CHUNKS = 8
HEAD_DIM = 128
MAX_REL = 128
SPATIAL = 128
N_MOD = 6
EPS = 1e-6
NEG_INF = -1e30

Q_BLOCK = 2 * CHUNK
KEY_WINDOW = (LEFT_CHUNKS + 2) * CHUNK
LEFT_KEYS = LEFT_CHUNKS * CHUNK

VMEM_LIMIT_BYTES = 56 * 1024 * 1024

BF16 = jnp.bfloat16
F32 = jnp.float32


def _rms_scale(t):
    return lax.rsqrt(jnp.mean(t * t, axis=-1, keepdims=True) + EPS)


def _gelu(t):
    return 0.5 * t * (1.0 + lax.erf(t * (0.5 ** 0.5)))


def _group_norm(t, gain, group):
    parts = []
    for s in range(0, t.shape[-1], group):
        p = t[:, s:s + group]
        parts.append(p * _rms_scale(p) * gain[:, s:s + group])
    return jnp.concatenate(parts, axis=-1)


def _ada_kernel(c_ref, w_ref, b_ref, o_ref):
    c = c_ref[...]
    cond = (c * jax.nn.sigmoid(c)).astype(BF16)
    o_ref[...] = jnp.dot(cond, w_ref[...].astype(BF16), preferred_element_type=F32) + b_ref[...]


def _ada_mod(c, w_ada, b_ada, tn=1024):
    b, d = c.shape
    n = w_ada.shape[1]
    return pl.pallas_call(
        _ada_kernel,
        grid=(n // tn,),
        in_specs=[
            pl.BlockSpec((b, d), lambda j: (0, 0)),
            pl.BlockSpec((d, tn), lambda j: (0, j)),
            pl.BlockSpec((1, tn), lambda j: (0, j)),
        ],
        out_specs=pl.BlockSpec((b, tn), lambda j: (0, j)),
        out_shape=jax.ShapeDtypeStruct((b, n), F32),
        compiler_params=pltpu.CompilerParams(
            dimension_semantics=("arbitrary",), vmem_limit_bytes=VMEM_LIMIT_BYTES),
        name="ada_mod",
    )(c, w_ada, b_ada.reshape(1, n))


def _in_proj_kernel(x_ref, mod_ref, g_ref, w_ref, sg_ref, o_ref, h_ref, *, n_attn_sec):
    j = pl.program_id(1)

    @pl.when(j == 0)
    def _():
        x = x_ref[...]
        shift = mod_ref[0:1, :]
        scale = mod_ref[1:2, :]
        h = x * _rms_scale(x) * g_ref[...]
        h_ref[...] = (h * (1.0 + scale) + shift).astype(BF16)

    acc = jnp.dot(h_ref[...], w_ref[...], preferred_element_type=F32)
    gain = sg_ref[...]

    @pl.when(j < 2)
    def _():
        o_ref[...] = _group_norm(acc, gain, HEAD_DIM).astype(BF16)

    @pl.when(j == 2)
    def _():
        o_ref[...] = acc.astype(BF16)

    @pl.when(j == 3)
    def _():
        o_ref[...] = _gelu(acc).astype(BF16)

    @pl.when(j == 4)
    def _():
        o_ref[...] = _group_norm(_gelu(acc), gain, HEAD_DIM).astype(BF16)


def _in_proj(x2, mod3, g, w_in, sec_gain, seq, tm=512):
    t, d = x2.shape
    n = w_in.shape[1]
    tn = sec_gain.shape[-1]
    steps_per_batch = seq // tm
    return pl.pallas_call(
        functools.partial(_in_proj_kernel, n_attn_sec=3),
        grid=(t // tm, n // tn),
        in_specs=[
            pl.BlockSpec((tm, d), lambda i, j: (i, 0)),
            pl.BlockSpec((None, N_MOD, d), lambda i, j: (i // steps_per_batch, 0, 0)),
            pl.BlockSpec((1, d), lambda i, j: (0, 0)),
            pl.BlockSpec((d, tn), lambda i, j: (0, j)),
            pl.BlockSpec((None, 1, tn), lambda i, j: (j, 0, 0)),
        ],
        out_specs=pl.BlockSpec((tm, tn), lambda i, j: (i, j)),
        out_shape=jax.ShapeDtypeStruct((t, n), BF16),
        scratch_shapes=[pltpu.VMEM((tm, d), BF16)],
        compiler_params=pltpu.CompilerParams(
            dimension_semantics=("arbitrary", "arbitrary"), vmem_limit_bytes=VMEM_LIMIT_BYTES),
        name="in_proj",
    )(x2, mod3, g, w_in, sec_gain)


def _attn_kernel(q_ref, k_ref, v_ref, bias_ref, o_ref, *, seq):
    scale = HEAD_DIM ** -0.5
    for qb in range(seq // Q_BLOCK):
        p0 = qb * Q_BLOCK
        ws = max(0, p0 - LEFT_KEYS)
        co = ws - (p0 - LEFT_KEYS)
        q = q_ref[p0:p0 + Q_BLOCK, :]
        kw = k_ref[ws:p0 + Q_BLOCK, :]
        vw = v_ref[ws:p0 + Q_BLOCK, :]
        s = lax.dot_general(q, kw, (((1,), (1,)), ((), ())), preferred_element_type=F32)
        s = s * scale + bias_ref[:, co:KEY_WINDOW]
        m = jnp.max(s, axis=-1, keepdims=True)
        p = jnp.exp(s - m)
        l = jnp.sum(p, axis=-1, keepdims=True)
        o = jnp.dot(p.astype(BF16), vw, preferred_element_type=F32)
        o_ref[p0:p0 + Q_BLOCK, :] = (o / l).astype(BF16)


def _band_bias(rel_bias):
    r = jnp.arange(Q_BLOCK)[:, None]
    col = jnp.arange(KEY_WINDOW)[None, :]
    rel = jnp.clip(r + LEFT_KEYS - col, -MAX_REL, MAX_REL) + MAX_REL
    qc = r // CHUNK
    kc = col // CHUNK - LEFT_CHUNKS
    visible = (kc <= qc) & (kc >= qc - LEFT_CHUNKS)
    return jnp.where(visible[None], rel_bias.astype(F32)[:, rel], NEG_INF)


def _band_attn(proj3, bias, n_heads):
    b, seq, _ = proj3.shape
    blk = lambda off: pl.BlockSpec((None, seq, HEAD_DIM), lambda bi, h: (bi, 0, off + h))
    return pl.pallas_call(
        functools.partial(_attn_kernel, seq=seq),
        grid=(b, n_heads),
        in_specs=[
            blk(0), blk(n_heads), blk(2 * n_heads),
            pl.BlockSpec((None, Q_BLOCK, KEY_WINDOW), lambda bi, h: (h, 0, 0)),
        ],
        out_specs=pl.BlockSpec((None, seq, HEAD_DIM), lambda bi, h: (bi, 0, h)),
        out_shape=jax.ShapeDtypeStruct((b, seq, n_heads * HEAD_DIM), BF16),
        compiler_params=pltpu.CompilerParams(
            dimension_semantics=("arbitrary", "arbitrary"), vmem_limit_bytes=VMEM_LIMIT_BYTES),
        name="band_attn",
    )(proj3, proj3, proj3, bias)


def _gate_kernel(u_ref, v_ref, w_ref, b_ref, o_ref, *, seq):
    t = lax.broadcasted_iota(jnp.int32, (SPATIAL, SPATIAL), 0)
    s = lax.broadcasted_iota(jnp.int32, (SPATIAL, SPATIAL), 1)
    w = jnp.where((t // CHUNK) >= (s // CHUNK), w_ref[...], 0.0).astype(BF16)
    bias = b_ref[...]
    for n in range(seq // SPATIAL):
        rows = slice(n * SPATIAL, (n + 1) * SPATIAL)
        y = jnp.dot(w, v_ref[rows, :], preferred_element_type=F32) + bias
        o_ref[rows, :] = (u_ref[rows, :].astype(F32) * y).astype(BF16)


def _spatial_gate(proj3, w_spatial, b_spatial, u_off, v_off):
    b, seq, _ = proj3.shape
    g = w_spatial.shape[0]
    width = w_spatial.shape[0] * HEAD_DIM
    return pl.pallas_call(
        functools.partial(_gate_kernel, seq=seq),
        grid=(b, g),
        in_specs=[
            pl.BlockSpec((None, seq, HEAD_DIM), lambda bi, gi: (bi, 0, u_off + gi)),
            pl.BlockSpec((None, seq, HEAD_DIM), lambda bi, gi: (bi, 0, v_off + gi)),
            pl.BlockSpec((None, SPATIAL, SPATIAL), lambda bi, gi: (gi, 0, 0)),
            pl.BlockSpec((None, SPATIAL, 1), lambda bi, gi: (gi, 0, 0)),
        ],
        out_specs=pl.BlockSpec((None, seq, HEAD_DIM), lambda bi, gi: (bi, 0, gi)),
        out_shape=jax.ShapeDtypeStruct((b, seq, width), BF16),
        compiler_params=pltpu.CompilerParams(
            dimension_semantics=("arbitrary", "arbitrary"), vmem_limit_bytes=VMEM_LIMIT_BYTES),
        name="spatial_gate",
    )(proj3, proj3, w_spatial, b_spatial.reshape(g, SPATIAL, 1))


def _out_proj_kernel(a_ref, m_ref, x_ref, mod_ref, ga_ref, gg_ref, gf_ref, w_ref, x1_ref, h2_ref):
    a = a_ref[...].astype(F32)
    m = m_ref[...].astype(F32)
    mix = jnp.concatenate(
        [(a * _rms_scale(a) * ga_ref[...]).astype(BF16),
         (m * _rms_scale(m) * gg_ref[...]).astype(BF16)], axis=-1)
    y = jnp.dot(mix, w_ref[...], preferred_element_type=F32)
    gate_m = mod_ref[2:3, :]
    shift_f = mod_ref[3:4, :]
    scale_f = mod_ref[4:5, :]
    x1 = x_ref[...] + gate_m * y
    x1_ref[...] = x1
    h2 = x1 * _rms_scale(x1) * gf_ref[...]
    h2_ref[...] = (h2 * (1.0 + scale_f) + shift_f).astype(BF16)


def _out_proj(attn2, gm2, x2, mod3, ga, gg, gf, w_out, seq, tm=512):
    t, d = x2.shape
    wa = attn2.shape[1]
    wg = gm2.shape[1]
    steps_per_batch = seq // tm
    return pl.pallas_call(
        _out_proj_kernel,
        grid=(t // tm,),
        in_specs=[
            pl.BlockSpec((tm, wa), lambda i: (i, 0)),
            pl.BlockSpec((tm, wg), lambda i: (i, 0)),
            pl.BlockSpec((tm, d), lambda i: (i, 0)),
            pl.BlockSpec((None, N_MOD, d), lambda i: (i // steps_per_batch, 0, 0)),
            pl.BlockSpec((1, wa), lambda i: (0, 0)),
            pl.BlockSpec((1, wg), lambda i: (0, 0)),
            pl.BlockSpec((1, d), lambda i: (0, 0)),
            pl.BlockSpec((wa + wg, d), lambda i: (0, 0)),
        ],
        out_specs=[
            pl.BlockSpec((tm, d), lambda i: (i, 0)),
            pl.BlockSpec((tm, d), lambda i: (i, 0)),
        ],
        out_shape=[jax.ShapeDtypeStruct((t, d), F32), jax.ShapeDtypeStruct((t, d), BF16)],
        compiler_params=pltpu.CompilerParams(
            dimension_semantics=("arbitrary",), vmem_limit_bytes=VMEM_LIMIT_BYTES),
        name="out_proj",
    )(attn2, gm2, x2, mod3, ga, gg, gf, w_out)


def _ffn_kernel(h_ref, w1_ref, w2_ref, x1_ref, mod_ref, o_ref):
    j = pl.program_id(1)
    a = jnp.dot(h_ref[...], w1_ref[...], preferred_element_type=F32)
    a = jnp.square(jnp.maximum(a, 0.0)).astype(BF16)
    y = jnp.dot(a, w2_ref[...], preferred_element_type=F32)

    @pl.when(j == 0)
    def _():
        o_ref[...] = y

    @pl.when(j > 0)
    def _():
        o_ref[...] += y

    @pl.when(j == pl.num_programs(1) - 1)
    def _():
        gate_f = mod_ref[5:6, :]
        o_ref[...] = x1_ref[...] + gate_f * o_ref[...]


def _ffn(h2, w1, w2, x1, mod3, seq, tm=512, tf=1024):
    t, d = h2.shape
    f = w1.shape[1]
    steps_per_batch = seq // tm
    return pl.pallas_call(
        _ffn_kernel,
        grid=(t // tm, f // tf),
        in_specs=[
            pl.BlockSpec((tm, d), lambda i, j: (i, 0)),
            pl.BlockSpec((d, tf), lambda i, j: (0, j)),
            pl.BlockSpec((tf, d), lambda i, j: (j, 0)),
            pl.BlockSpec((tm, d), lambda i, j: (i, 0)),
            pl.BlockSpec((None, N_MOD, d), lambda i, j: (i // steps_per_batch, 0, 0)),
        ],
        out_specs=pl.BlockSpec((tm, d), lambda i, j: (i, 0)),
        out_shape=jax.ShapeDtypeStruct((t, d), F32),
        compiler_params=pltpu.CompilerParams(
            dimension_semantics=("arbitrary", "arbitrary"), vmem_limit_bytes=VMEM_LIMIT_BYTES),
        name="ffn",
    )(h2, w1, w2, x1, mod3)


def kernel(x, c, w_ada, b_ada, mix_norm_g, w_in, q_norm_g, k_norm_g, rel_bias, gmlp_norm_g,
           w_spatial, b_spatial, attn_out_g, gmlp_out_g, w_out, ff_norm_g, w_ff1, w_ff2):
    b, seq, d = x.shape
    depth = w_ada.shape[0]
    n_heads = rel_bias.shape[1]
    n_groups = w_spatial.shape[1]
    attn_w = n_heads * HEAD_DIM
    gmlp_w = n_groups * HEAD_DIM
    assert attn_w == gmlp_w and w_in.shape[2] == 3 * attn_w + 2 * gmlp_w
    ones = jnp.ones((attn_w,), F32)

    x2 = x.reshape(b * seq, d)
    for l in range(depth):
        mod3 = _ada_mod(c, w_ada[l], b_ada[l]).reshape(b, N_MOD, d)
        sec_gain = jnp.stack([
            jnp.tile(q_norm_g[l], n_heads), jnp.tile(k_norm_g[l], n_heads), ones, ones,
            gmlp_norm_g[l].reshape(-1)]).reshape(5, 1, attn_w).astype(F32)
        proj = _in_proj(x2, mod3, mix_norm_g[l].reshape(1, d), w_in[l].astype(BF16), sec_gain, seq)
        proj3 = proj.reshape(b, seq, -1)
        attn = _band_attn(proj3, _band_bias(rel_bias[l]), n_heads)
        gm = _spatial_gate(proj3, w_spatial[l], b_spatial[l], 3 * n_heads, 3 * n_heads + n_groups)
        x1, h2 = _out_proj(
            attn.reshape(b * seq, attn_w), gm.reshape(b * seq, gmlp_w), x2, mod3,
            attn_out_g[l].reshape(1, attn_w), gmlp_out_g[l].reshape(1, gmlp_w),
            ff_norm_g[l].reshape(1, d), w_out[l].astype(BF16), seq)
        x2 = _ffn(h2, w_ff1[l].astype(BF16), w_ff2[l].astype(BF16), x1, mod3, seq)
    return x2.reshape(b, seq, d)
```

```python
import functools

import jax
import jax.numpy as jnp
from jax import lax
from jax.experimental import pallas as pl
from jax.experimental.pallas import tpu as pltpu

CHUNK = 64
LEFT_CHUNKS = 8
HEAD_DIM = 128
MAX_REL = 128
SPATIAL = 128
N_MOD = 6
EPS = 1e-6
NEG_INF = -1e30

Q_BLOCK = 2 * CHUNK
KEY_WINDOW = (LEFT_CHUNKS + 2) * CHUNK
LEFT_KEYS = LEFT_CHUNKS * CHUNK
ROLL_WIDTH = 1024

VMEM_LIMIT_BYTES = 56 * 1024 * 1024

BF16 = jnp.bfloat16
F32 = jnp.float32


def _rms_scale(t):
    return lax.rsqrt(jnp.mean(t * t, axis=-1, keepdims=True) + EPS)


def _gelu(t):
    return 0.5 * t * (1.0 + lax.erf(t * (0.5 ** 0.5)))


def _group_norm(t, gain, group):
    parts = []
    for s in range(0, t.shape[-1], group):
        p = t[:, s:s + group]
        parts.append(p * _rms_scale(p) * gain[:, s:s + group])
    return jnp.concatenate(parts, axis=-1)


def _ada_kernel(c_ref, w_ref, b_ref, o_ref):
    c = c_ref[...]
    cond = (c * jax.nn.sigmoid(c)).astype(BF16)
    o_ref[...] = jnp.dot(cond, w_ref[...].astype(BF16), preferred_element_type=F32) + b_ref[...]


def _ada_mod(c, w_ada, b_ada, tn=1024):
    b, d = c.shape
    n = w_ada.shape[1]
    return pl.pallas_call(
        _ada_kernel,
        grid=(n // tn,),
        in_specs=[
            pl.BlockSpec((b, d), lambda j: (0, 0)),
            pl.BlockSpec((d, tn), lambda j: (0, j)),
            pl.BlockSpec((1, tn), lambda j: (0, j)),
        ],
        out_specs=pl.BlockSpec((b, tn), lambda j: (0, j)),
        out_shape=jax.ShapeDtypeStruct((b, n), F32),
        compiler_params=pltpu.CompilerParams(
            dimension_semantics=("arbitrary",), vmem_limit_bytes=VMEM_LIMIT_BYTES),
        name="ada_mod",
    )(c, w_ada, b_ada.reshape(1, n))


def _in_proj_kernel(x_ref, mod_ref, g_ref, w_ref, sg_ref, o_ref, *, sec):
    x = x_ref[...]
    shift = mod_ref[0:1, :]
    scale = mod_ref[1:2, :]
    h = x * _rms_scale(x) * g_ref[...]
    h = (h * (1.0 + scale) + shift).astype(BF16)

    def section(s):
        return jnp.dot(h, w_ref[:, s * sec:(s + 1) * sec], preferred_element_type=F32)

    def put(s, val):
        o_ref[:, s * sec:(s + 1) * sec] = val.astype(BF16)

    put(0, _group_norm(section(0), sg_ref[0:1, :], HEAD_DIM))
    put(1, _group_norm(section(1), sg_ref[1:2, :], HEAD_DIM))
    put(2, section(2))
    put(3, _gelu(section(3)))
    put(4, _group_norm(_gelu(section(4)), sg_ref[2:3, :], HEAD_DIM))


def _in_proj(x2, mod3, g, w_in, sec_gain, seq, tm=256):
    t, d = x2.shape
    n = w_in.shape[1]
    sec = sec_gain.shape[-1]
    steps_per_batch = seq // tm
    return pl.pallas_call(
        functools.partial(_in_proj_kernel, sec=sec),
        grid=(t // tm,),
        in_specs=[
            pl.BlockSpec((tm, d), lambda i: (i, 0)),
            pl.BlockSpec((None, N_MOD, d), lambda i: (i // steps_per_batch, 0, 0)),
            pl.BlockSpec((1, d), lambda i: (0, 0)),
            pl.BlockSpec((d, n), lambda i: (0, 0), pipeline_mode=pl.Buffered(1)),
            pl.BlockSpec(sec_gain.shape, lambda i: (0, 0)),
        ],
        out_specs=pl.BlockSpec((tm, n), lambda i: (i, 0)),
        out_shape=jax.ShapeDtypeStruct((t, n), BF16),
        compiler_params=pltpu.CompilerParams(
            dimension_semantics=("arbitrary",), vmem_limit_bytes=VMEM_LIMIT_BYTES),
        name="in_proj",
    )(x2, mod3, g, w_in, sec_gain)


def _attn_kernel(q_ref, k_ref, v_ref, base_ref, o_ref, bias_ref, *, seq):
    scale = HEAD_DIM ** -0.5
    tab = pltpu.roll(jnp.broadcast_to(base_ref[...], (Q_BLOCK, ROLL_WIDTH)), 0, 1,
                     stride=1, stride_axis=0)[:, :KEY_WINDOW]
    r = lax.broadcasted_iota(jnp.int32, (Q_BLOCK, KEY_WINDOW), 0)
    col = lax.broadcasted_iota(jnp.int32, (Q_BLOCK, KEY_WINDOW), 1)
    qc = r // CHUNK
    kc = col // CHUNK - LEFT_CHUNKS
    bias_ref[...] = jnp.where((kc <= qc) & (kc >= qc - LEFT_CHUNKS), tab, NEG_INF)
    for qb in range(seq // Q_BLOCK):
        p0 = qb * Q_BLOCK
        ws = max(0, p0 - LEFT_KEYS)
        co = ws - (p0 - LEFT_KEYS)
        q = q_ref[p0:p0 + Q_BLOCK, :]
        kw = k_ref[ws:p0 + Q_BLOCK, :]
        vw = v_ref[ws:p0 + Q_BLOCK, :]
        s = lax.dot_general(q, kw, (((1,), (1,)), ((), ())), preferred_element_type=F32)
        s = s * scale + bias_ref[:, co:KEY_WINDOW]
        m = jnp.max(s, axis=-1, keepdims=True)
        p = jnp.exp(s - m)
        l = jnp.sum(p, axis=-1, keepdims=True)
        o = jnp.dot(p.astype(BF16), vw, preferred_element_type=F32)
        o_ref[p0:p0 + Q_BLOCK, :] = (o / l).astype(BF16)


def _bias_row(rel_bias):
    rb = rel_bias.astype(F32)
    h = rb.shape[0]
    far = rb[:, 2 * MAX_REL:]
    row = jnp.concatenate([
        jnp.broadcast_to(far, (h, LEFT_KEYS - MAX_REL + 1)),
        rb[:, 1:2 * MAX_REL][:, ::-1],
        jnp.broadcast_to(far, (h, ROLL_WIDTH - LEFT_KEYS - MAX_REL)),
    ], axis=1)
    return row.reshape(h, 1, ROLL_WIDTH)


def _band_attn(proj3, bias_row, n_heads):
    b, seq, _ = proj3.shape
    blk = lambda off: pl.BlockSpec((None, seq, HEAD_DIM), lambda bi, h: (bi, 0, off + h))
    return pl.pallas_call(
        functools.partial(_attn_kernel, seq=seq),
        grid=(b, n_heads),
        in_specs=[
            blk(0), blk(n_heads), blk(2 * n_heads),
            pl.BlockSpec((None, 1, ROLL_WIDTH), lambda bi, h: (h, 0, 0)),
        ],
        out_specs=pl.BlockSpec((None, seq, HEAD_DIM), lambda bi, h: (bi, 0, h)),
        out_shape=jax.ShapeDtypeStruct((b, seq, n_heads * HEAD_DIM), BF16),
        scratch_shapes=[pltpu.VMEM((Q_BLOCK, KEY_WINDOW), F32)],
        compiler_params=pltpu.CompilerParams(
            dimension_semantics=("arbitrary", "arbitrary"), vmem_limit_bytes=VMEM_LIMIT_BYTES),
        name="band_attn",
    )(proj3, proj3, proj3, bias_row)


def _gate_kernel(u_ref, v_ref, w_ref, b_ref, o_ref, *, seq):
    t = lax.broadcasted_iota(jnp.int32, (SPATIAL, SPATIAL), 0)
    s = lax.broadcasted_iota(jnp.int32, (SPATIAL, SPATIAL), 1)
    w = jnp.where((t // CHUNK) >= (s // CHUNK), w_ref[...], 0.0).astype(BF16)
    bias = b_ref[...]
    for n in range(seq // SPATIAL):
        rows = slice(n * SPATIAL, (n + 1) * SPATIAL)
        y = jnp.dot(w, v_ref[rows, :], preferred_element_type=F32) + bias
        o_ref[rows, :] = (u_ref[rows, :].astype(F32) * y).astype(BF16)


def _spatial_gate(proj3, w_spatial, b_spatial, u_off, v_off):
    b, seq, _ = proj3.shape
    g = w_spatial.shape[0]
    width = w_spatial.shape[0] * HEAD_DIM
    return pl.pallas_call(
        functools.partial(_gate_kernel, seq=seq),
        grid=(b, g),
        in_specs=[
            pl.BlockSpec((None, seq, HEAD_DIM), lambda bi, gi: (bi, 0, u_off + gi)),
            pl.BlockSpec((None, seq, HEAD_DIM), lambda bi, gi: (bi, 0, v_off + gi)),
            pl.BlockSpec((None, SPATIAL, SPATIAL), lambda bi, gi: (gi, 0, 0)),
            pl.BlockSpec((None, SPATIAL, 1), lambda bi, gi: (gi, 0, 0)),
        ],
        out_specs=pl.BlockSpec((None, seq, HEAD_DIM), lambda bi, gi: (bi, 0, gi)),
        out_shape=jax.ShapeDtypeStruct((b, seq, width), BF16),
        compiler_params=pltpu.CompilerParams(
            dimension_semantics=("arbitrary", "arbitrary"), vmem_limit_bytes=VMEM_LIMIT_BYTES),
        name="spatial_gate",
    )(proj3, proj3, w_spatial, b_spatial.reshape(g, SPATIAL, 1))


def _out_proj_kernel(a_ref, m_ref, x_ref, mod_ref, ga_ref, gg_ref, gf_ref, w_ref, x1_ref, h2_ref):
    a = a_ref[...].astype(F32)
    m = m_ref[...].astype(F32)
    mix = jnp.concatenate(
        [(a * _rms_scale(a) * ga_ref[...]).astype(BF16),
         (m * _rms_scale(m) * gg_ref[...]).astype(BF16)], axis=-1)
    y = jnp.dot(mix, w_ref[...], preferred_element_type=F32)
    gate_m = mod_ref[2:3, :]
    shift_f = mod_ref[3:4, :]
    scale_f = mod_ref[4:5, :]
    x1 = x_ref[...] + gate_m * y
    x1_ref[...] = x1
    h2 = x1 * _rms_scale(x1) * gf_ref[...]
    h2_ref[...] = (h2 * (1.0 + scale_f) + shift_f).astype(BF16)


def _out_proj(attn2, gm2, x2, mod3, ga, gg, gf, w_out, seq, tm=512):
    t, d = x2.shape
    wa = attn2.shape[1]
    wg = gm2.shape[1]
    steps_per_batch = seq // tm
    return pl.pallas_call(
        _out_proj_kernel,
        grid=(t // tm,),
        in_specs=[
            pl.BlockSpec((tm, wa), lambda i: (i, 0)),
            pl.BlockSpec((tm, wg), lambda i: (i, 0)),
            pl.BlockSpec((tm, d), lambda i: (i, 0)),
            pl.BlockSpec((None, N_MOD, d), lambda i: (i // steps_per_batch, 0, 0)),
            pl.BlockSpec((1, wa), lambda i: (0, 0)),
            pl.BlockSpec((1, wg), lambda i: (0, 0)),
            pl.BlockSpec((1, d), lambda i: (0, 0)),
            pl.BlockSpec((wa + wg, d), lambda i: (0, 0)),
        ],
        out_specs=[
            pl.BlockSpec((tm, d), lambda i: (i, 0)),
            pl.BlockSpec((tm, d), lambda i: (i, 0)),
        ],
        out_shape=[jax.ShapeDtypeStruct((t, d), F32), jax.ShapeDtypeStruct((t, d), BF16)],
        compiler_params=pltpu.CompilerParams(
            dimension_semantics=("arbitrary",), vmem_limit_bytes=VMEM_LIMIT_BYTES),
        name="out_proj",
    )(attn2, gm2, x2, mod3, ga, gg, gf, w_out)


def _ffn_kernel(h_ref, w1_ref, w2_ref, x1_ref, mod_ref, o_ref):
    j = pl.program_id(1)

    @pl.when(j == 0)
    def _():
        o_ref[...] = jnp.zeros_like(o_ref)

    a = jnp.dot(h_ref[...], w1_ref[...], preferred_element_type=F32)
    a = jnp.square(jnp.maximum(a, 0.0)).astype(BF16)
    o_ref[...] += jnp.dot(a, w2_ref[...], preferred_element_type=F32)

    @pl.when(j == pl.num_programs(1) - 1)
    def _():
        gate_f = mod_ref[5:6, :]
        o_ref[...] = x1_ref[...] + gate_f * o_ref[...]


def _ffn(h2, w1, w2, x1, mod3, seq, tm=512, tf=1024):
    t, d = h2.shape
    f = w1.shape[1]
    steps_per_batch = seq // tm
    return pl.pallas_call(
        _ffn_kernel,
        grid=(t // tm, f // tf),
        in_specs=[
            pl.BlockSpec((tm, d), lambda i, j: (i, 0)),
            pl.BlockSpec((d, tf), lambda i, j: (0, j)),
            pl.BlockSpec((tf, d), lambda i, j: (j, 0)),
            pl.BlockSpec((tm, d), lambda i, j: (i, 0)),
            pl.BlockSpec((None, N_MOD, d), lambda i, j: (i // steps_per_batch, 0, 0)),
        ],
        out_specs=pl.BlockSpec((tm, d), lambda i, j: (i, 0)),
        out_shape=jax.ShapeDtypeStruct((t, d), F32),
        compiler_params=pltpu.CompilerParams(
            dimension_semantics=("arbitrary", "arbitrary"), vmem_limit_bytes=VMEM_LIMIT_BYTES),
        name="ffn",
    )(h2, w1, w2, x1, mod3)


def kernel(x, c, w_ada, b_ada, mix_norm_g, w_in, q_norm_g, k_norm_g, rel_bias, gmlp_norm_g,
           w_spatial, b_spatial, attn_out_g, gmlp_out_g, w_out, ff_norm_g, w_ff1, w_ff2):
    b, seq, d = x.shape
    depth = w_ada.shape[0]
    n_heads = rel_bias.shape[1]
    n_groups = w_spatial.shape[1]
    attn_w = n_heads * HEAD_DIM
    gmlp_w = n_groups * HEAD_DIM
    assert attn_w == gmlp_w and w_in.shape[2] == 3 * attn_w + 2 * gmlp_w
    ones = jnp.ones((attn_w,), F32)

    x2 = x.reshape(b * seq, d)
    for l in range(depth):
        mod3 = _ada_mod(c, w_ada[l], b_ada[l]).reshape(b, N_MOD, d)
        sec_gain = jnp.stack([
            jnp.tile(q_norm_g[l], n_heads), jnp.tile(k_norm_g[l], n_heads),
            gmlp_norm_g[l].reshape(-1)]).astype(F32)
        proj = _in_proj(x2, mod3, mix_norm_g[l].reshape(1, d), w_in[l].astype(BF16), sec_gain, seq)
        proj3 = proj.reshape(b, seq, -1)
        attn = _band_attn(proj3, _bias_row(rel_bias[l]), n_heads)
        gm = _spatial_gate(proj3, w_spatial[l], b_spatial[l], 3 * n_heads, 3 * n_heads + n_groups)
        x1, h2 = _out_proj(
            attn.reshape(b * seq, attn_w), gm.reshape(b * seq, gmlp_w), x2, mod3,
            attn_out_g[l].reshape(1, attn_w), gmlp_out_g[l].reshape(1, gmlp_w),
            ff_norm_g[l].reshape(1, d), w_out[l].astype(BF16), seq)
        x2 = _ffn(h2, w_ff1[l].astype(BF16), w_ff2[l].astype(BF16), x1, mod3, seq)
    return x2.reshape(b, seq, d)
```

```python
import functools

import jax
import jax.numpy as jnp
from jax import lax
from jax.experimental import pallas as pl
from jax.experimental.pallas import tpu as pltpu

CHUNK = 64
LEFT_CHUNKS = 8
HEAD_DIM = 128
MAX_REL = 128
SPATIAL = 128
N_MOD = 6
EPS = 1e-6
NEG_INF = -1e30

Q_BLOCK = 2 * CHUNK
KEY_WINDOW = (LEFT_CHUNKS + 2) * CHUNK
LEFT_KEYS = LEFT_CHUNKS * CHUNK
ROLL_WIDTH = 1024
SCORE_LOOKAHEAD = 2

VMEM_LIMIT_BYTES = 56 * 1024 * 1024

BF16 = jnp.bfloat16
F32 = jnp.float32


def _rms_scale(t):
    return lax.rsqrt(jnp.mean(t * t, axis=-1, keepdims=True) + EPS)


def _gelu(t):
    return 0.5 * t * (1.0 + lax.erf(t * (0.5 ** 0.5)))


def _group_norm(t, gain, group):
    parts = []
    for s in range(0, t.shape[-1], group):
        p = t[:, s:s + group]
        parts.append(p * _rms_scale(p) * gain[:, s:s + group])
    return jnp.concatenate(parts, axis=-1)


def _ada_kernel(c_ref, w_ref, b_ref, o_ref):
    c = c_ref[...]
    cond = (c * jax.nn.sigmoid(c)).astype(BF16)
    o_ref[...] = jnp.dot(cond, w_ref[...].astype(BF16), preferred_element_type=F32) + b_ref[...]


def _ada_mod(c, w_ada, b_ada, tn=1024):
    b, d = c.shape
    n = w_ada.shape[1]
    return pl.pallas_call(
        _ada_kernel,
        grid=(n // tn,),
        in_specs=[
            pl.BlockSpec((b, d), lambda j: (0, 0)),
            pl.BlockSpec((d, tn), lambda j: (0, j)),
            pl.BlockSpec((1, tn), lambda j: (0, j)),
        ],
        out_specs=pl.BlockSpec((b, tn), lambda j: (0, j)),
        out_shape=jax.ShapeDtypeStruct((b, n), F32),
        compiler_params=pltpu.CompilerParams(
            dimension_semantics=("arbitrary",), vmem_limit_bytes=VMEM_LIMIT_BYTES),
        name="ada_mod",
    )(c, w_ada, b_ada.reshape(1, n))


def _in_proj_kernel(x_ref, mod_ref, g_ref, w_ref, sg_ref, o_ref, *, sec):
    x = x_ref[...]
    shift = mod_ref[0:1, :]
    scale = mod_ref[1:2, :]
    h = x * _rms_scale(x) * g_ref[...]
    h = (h * (1.0 + scale) + shift).astype(BF16)

    def section(s):
        return jnp.dot(h, w_ref[:, s * sec:(s + 1) * sec], preferred_element_type=F32)

    def put(s, val):
        o_ref[:, s * sec:(s + 1) * sec] = val.astype(BF16)

    put(0, _group_norm(section(0), sg_ref[0:1, :], HEAD_DIM))
    put(1, _group_norm(section(1), sg_ref[1:2, :], HEAD_DIM))
    put(2, section(2))
    put(3, _gelu(section(3)))
    put(4, _group_norm(_gelu(section(4)), sg_ref[2:3, :], HEAD_DIM))


def _in_proj(x2, mod3, g, w_in, sec_gain, seq, tm=256):
    t, d = x2.shape
    n = w_in.shape[1]
    sec = sec_gain.shape[-1]
    steps_per_batch = seq // tm
    return pl.pallas_call(
        functools.partial(_in_proj_kernel, sec=sec),
        grid=(t // tm,),
        in_specs=[
            pl.BlockSpec((tm, d), lambda i: (i, 0)),
            pl.BlockSpec((None, N_MOD, d), lambda i: (i // steps_per_batch, 0, 0)),
            pl.BlockSpec((1, d), lambda i: (0, 0)),
            pl.BlockSpec((d, n), lambda i: (0, 0), pipeline_mode=pl.Buffered(1)),
            pl.BlockSpec(sec_gain.shape, lambda i: (0, 0)),
        ],
        out_specs=pl.BlockSpec((tm, n), lambda i: (i, 0)),
        out_shape=jax.ShapeDtypeStruct((t, n), BF16),
        compiler_params=pltpu.CompilerParams(
            dimension_semantics=("arbitrary",), vmem_limit_bytes=VMEM_LIMIT_BYTES),
        name="in_proj",
    )(x2, mod3, g, w_in, sec_gain)


def _mixer_kernel(q_ref, k_ref, v_ref, base_ref, u_ref, z_ref, ws_ref, bs_ref,
                  o_ref, gm_ref, bias_ref, *, seq):
    _attention(q_ref, k_ref, v_ref, base_ref, o_ref, bias_ref, seq)
    _spatial_gate(u_ref, z_ref, ws_ref, bs_ref, gm_ref, seq)


def _spatial_gate(u_ref, z_ref, w_ref, b_ref, o_ref, seq):
    t = lax.broadcasted_iota(jnp.int32, (SPATIAL, SPATIAL), 0)
    s = lax.broadcasted_iota(jnp.int32, (SPATIAL, SPATIAL), 1)
    w = jnp.where((t // CHUNK) >= (s // CHUNK), w_ref[...], 0.0).astype(BF16)
    bias = b_ref[...]
    for n in range(seq // SPATIAL):
        rows = slice(n * SPATIAL, (n + 1) * SPATIAL)
        y = jnp.dot(w, z_ref[rows, :], preferred_element_type=F32) + bias
        o_ref[rows, :] = (u_ref[rows, :].astype(F32) * y).astype(BF16)


def _attention(q_ref, k_ref, v_ref, base_ref, o_ref, bias_ref, seq):
    scale = HEAD_DIM ** -0.5
    tab = pltpu.roll(jnp.broadcast_to(base_ref[...], (Q_BLOCK, ROLL_WIDTH)), 0, 1,
                     stride=1, stride_axis=0)[:, :KEY_WINDOW]
    r = lax.broadcasted_iota(jnp.int32, (Q_BLOCK, KEY_WINDOW), 0)
    col = lax.broadcasted_iota(jnp.int32, (Q_BLOCK, KEY_WINDOW), 1)
    qc = r // CHUNK
    kc = col // CHUNK - LEFT_CHUNKS
    bias_ref[...] = jnp.where((kc <= qc) & (kc >= qc - LEFT_CHUNKS), tab, NEG_INF)

    def window(qb):
        p0 = qb * Q_BLOCK
        ws = max(0, p0 - LEFT_KEYS)
        return p0, ws, ws - (p0 - LEFT_KEYS)

    def scores(qb):
        p0, ws, co = window(qb)
        s = lax.dot_general(q_ref[p0:p0 + Q_BLOCK, :], k_ref[ws:p0 + Q_BLOCK, :],
                            (((1,), (1,)), ((), ())), preferred_element_type=F32)
        return s * scale + bias_ref[:, co:KEY_WINDOW]

    def finish(qb, s):
        p0, ws, _ = window(qb)
        m = jnp.max(s, axis=-1, keepdims=True)
        p = jnp.exp(s - m)
        l = jnp.sum(p, axis=-1, keepdims=True)
        o = jnp.dot(p.astype(BF16), v_ref[ws:p0 + Q_BLOCK, :], preferred_element_type=F32)
        o_ref[p0:p0 + Q_BLOCK, :] = (o / l).astype(BF16)

    n_blocks = seq // Q_BLOCK
    pending = [scores(qb) for qb in range(SCORE_LOOKAHEAD)]
    for qb in range(n_blocks):
        if qb + SCORE_LOOKAHEAD < n_blocks:
            pending.append(scores(qb + SCORE_LOOKAHEAD))
        finish(qb, pending.pop(0))


def _bias_row(rel_bias):
    rb = rel_bias.astype(F32)
    h = rb.shape[0]
    far = rb[:, 2 * MAX_REL:]
    row = jnp.concatenate([
        jnp.broadcast_to(far, (h, LEFT_KEYS - MAX_REL + 1)),
        rb[:, 1:2 * MAX_REL][:, ::-1],
        jnp.broadcast_to(far, (h, ROLL_WIDTH - LEFT_KEYS - MAX_REL)),
    ], axis=1)
    return row.reshape(h, 1, ROLL_WIDTH)


def _mixers(proj3, bias_row, w_spatial, b_spatial):
    b, seq, _ = proj3.shape
    n = w_spatial.shape[0]
    blk = lambda sec: pl.BlockSpec((None, seq, HEAD_DIM), lambda bi, h: (bi, 0, sec * n + h))
    out_blk = pl.BlockSpec((None, seq, HEAD_DIM), lambda bi, h: (bi, 0, h))
    out = jax.ShapeDtypeStruct((b, seq, n * HEAD_DIM), BF16)
    return pl.pallas_call(
        functools.partial(_mixer_kernel, seq=seq),
        grid=(b, n),
        in_specs=[
            blk(0), blk(1), blk(2),
            pl.BlockSpec((None, 1, ROLL_WIDTH), lambda bi, h: (h, 0, 0)),
            blk(3), blk(4),
            pl.BlockSpec((None, SPATIAL, SPATIAL), lambda bi, h: (h, 0, 0)),
            pl.BlockSpec((None, SPATIAL, 1), lambda bi, h: (h, 0, 0)),
        ],
        out_specs=[out_blk, out_blk],
        out_shape=[out, out],
        scratch_shapes=[pltpu.VMEM((Q_BLOCK, KEY_WINDOW), F32)],
        compiler_params=pltpu.CompilerParams(
            dimension_semantics=("arbitrary", "arbitrary"), vmem_limit_bytes=VMEM_LIMIT_BYTES),
        name="mixers",
    )(proj3, proj3, proj3, bias_row, proj3, proj3, w_spatial,
      b_spatial.reshape(n, SPATIAL, 1))


def _out_proj_kernel(a_ref, m_ref, x_ref, mod_ref, ga_ref, gg_ref, gf_ref, w_ref, x1_ref, h2_ref,
                     *, row_chunk):
    gate_m = mod_ref[2:3, :]
    shift_f = mod_ref[3:4, :]
    scale_f = mod_ref[4:5, :]

    def mixed(r):
        rows = slice(r * row_chunk, (r + 1) * row_chunk)
        a = a_ref[rows, :].astype(F32)
        m = m_ref[rows, :].astype(F32)
        return jnp.concatenate(
            [(a * _rms_scale(a) * ga_ref[...]).astype(BF16),
             (m * _rms_scale(m) * gg_ref[...]).astype(BF16)], axis=-1)

    def project(mix):
        return jnp.dot(mix, w_ref[...], preferred_element_type=F32)

    def finish(r, y):
        rows = slice(r * row_chunk, (r + 1) * row_chunk)
        x1 = x_ref[rows, :] + gate_m * y
        x1_ref[rows, :] = x1
        h2 = x1 * _rms_scale(x1) * gf_ref[...]
        h2_ref[rows, :] = (h2 * (1.0 + scale_f) + shift_f).astype(BF16)

    n_chunks = a_ref.shape[0] // row_chunk
    mixes = [mixed(0)]
    ys = []
    for r in range(n_chunks + 1):
        if r + 1 < n_chunks:
            mixes.append(mixed(r + 1))
        if r < n_chunks:
            ys.append(project(mixes.pop(0)))
        if r >= 1:
            finish(r - 1, ys.pop(0))


def _out_proj(attn2, gm2, x2, mod3, ga, gg, gf, w_out, seq, tm=512, row_chunk=256):
    t, d = x2.shape
    wa = attn2.shape[1]
    wg = gm2.shape[1]
    steps_per_batch = seq // tm
    return pl.pallas_call(
        functools.partial(_out_proj_kernel, row_chunk=row_chunk),
        grid=(t // tm,),
        in_specs=[
            pl.BlockSpec((tm, wa), lambda i: (i, 0)),
            pl.BlockSpec((tm, wg), lambda i: (i, 0)),
            pl.BlockSpec((tm, d), lambda i: (i, 0)),
            pl.BlockSpec((None, N_MOD, d), lambda i: (i // steps_per_batch, 0, 0)),
            pl.BlockSpec((1, wa), lambda i: (0, 0)),
            pl.BlockSpec((1, wg), lambda i: (0, 0)),
            pl.BlockSpec((1, d), lambda i: (0, 0)),
            pl.BlockSpec((wa + wg, d), lambda i: (0, 0)),
        ],
        out_specs=[
            pl.BlockSpec((tm, d), lambda i: (i, 0)),
            pl.BlockSpec((tm, d), lambda i: (i, 0)),
        ],
        out_shape=[jax.ShapeDtypeStruct((t, d), F32), jax.ShapeDtypeStruct((t, d), BF16)],
        compiler_params=pltpu.CompilerParams(
            dimension_semantics=("arbitrary",), vmem_limit_bytes=VMEM_LIMIT_BYTES),
        name="out_proj",
    )(attn2, gm2, x2, mod3, ga, gg, gf, w_out)


def _ffn_kernel(h_ref, w1_ref, w2_ref, x1_ref, mod_ref, o_ref):
    j = pl.program_id(1)

    @pl.when(j == 0)
    def _():
        o_ref[...] = jnp.zeros_like(o_ref)

    a = jnp.dot(h_ref[...], w1_ref[...], preferred_element_type=F32)
    a = jnp.square(jnp.maximum(a, 0.0)).astype(BF16)
    o_ref[...] += jnp.dot(a, w2_ref[...], preferred_element_type=F32)

    @pl.when(j == pl.num_programs(1) - 1)
    def _():
        gate_f = mod_ref[5:6, :]
        o_ref[...] = x1_ref[...] + gate_f * o_ref[...]


def _ffn(h2, w1, w2, x1, mod3, seq, tm=512, tf=1024):
    t, d = h2.shape
    f = w1.shape[1]
    steps_per_batch = seq // tm
    return pl.pallas_call(
        _ffn_kernel,
        grid=(t // tm, f // tf),
        in_specs=[
            pl.BlockSpec((tm, d), lambda i, j: (i, 0)),
            pl.BlockSpec((d, tf), lambda i, j: (0, j)),
            pl.BlockSpec((tf, d), lambda i, j: (j, 0)),
            pl.BlockSpec((tm, d), lambda i, j: (i, 0)),
            pl.BlockSpec((None, N_MOD, d), lambda i, j: (i // steps_per_batch, 0, 0)),
        ],
        out_specs=pl.BlockSpec((tm, d), lambda i, j: (i, 0)),
        out_shape=jax.ShapeDtypeStruct((t, d), F32),
        compiler_params=pltpu.CompilerParams(
            dimension_semantics=("arbitrary", "arbitrary"), vmem_limit_bytes=VMEM_LIMIT_BYTES),
        name="ffn",
    )(h2, w1, w2, x1, mod3)


def kernel(x, c, w_ada, b_ada, mix_norm_g, w_in, q_norm_g, k_norm_g, rel_bias, gmlp_norm_g,
           w_spatial, b_spatial, attn_out_g, gmlp_out_g, w_out, ff_norm_g, w_ff1, w_ff2):
    b, seq, d = x.shape
    depth = w_ada.shape[0]
    n_heads = rel_bias.shape[1]
    n_groups = w_spatial.shape[1]
    attn_w = n_heads * HEAD_DIM
    gmlp_w = n_groups * HEAD_DIM
    assert n_heads == n_groups and w_in.shape[2] == 3 * attn_w + 2 * gmlp_w

    x2 = x.reshape(b * seq, d)
    for l in range(depth):
        mod3 = _ada_mod(c, w_ada[l], b_ada[l]).reshape(b, N_MOD, d)
        sec_gain = jnp.stack([
            jnp.tile(q_norm_g[l], n_heads), jnp.tile(k_norm_g[l], n_heads),
            gmlp_norm_g[l].reshape(-1)]).astype(F32)
        proj = _in_proj(x2, mod3, mix_norm_g[l].reshape(1, d), w_in[l].astype(BF16), sec_gain, seq)
        proj3 = proj.reshape(b, seq, -1)
        attn, gm = _mixers(proj3, _bias_row(rel_bias[l]), w_spatial[l], b_spatial[l])
        x1, h2 = _out_proj(
            attn.reshape(b * seq, attn_w), gm.reshape(b * seq, gmlp_w), x2, mod3,
            attn_out_g[l].reshape(1, attn_w), gmlp_out_g[l].reshape(1, gmlp_w),
            ff_norm_g[l].reshape(1, d), w_out[l].astype(BF16), seq)
        x2 = _ffn(h2, w_ff1[l].astype(BF16), w_ff2[l].astype(BF16), x1, mod3, seq)
    return x2.reshape(b, seq, d)
```

```python
import functools

import jax
import jax.numpy as jnp
from jax import lax
from jax.experimental import pallas as pl
from jax.experimental.pallas import tpu as pltpu

CHUNK = 64
LEFT_CHUNKS = 8
HEAD_DIM = 128
MAX_REL = 128
SPATIAL = 128
N_MOD = 6
EPS = 1e-6
NEG_INF = -1e30

Q_BLOCK = 2 * CHUNK
KEY_WINDOW = (LEFT_CHUNKS + 2) * CHUNK
LEFT_KEYS = LEFT_CHUNKS * CHUNK
ROLL_WIDTH = 1024
SCORE_LOOKAHEAD = 2

VMEM_LIMIT_BYTES = 56 * 1024 * 1024

BF16 = jnp.bfloat16
F32 = jnp.float32


def _rms_scale(t):
    return lax.rsqrt(jnp.mean(t * t, axis=-1, keepdims=True) + EPS)


def _gelu(t):
    return 0.5 * t * (1.0 + lax.erf(t * (0.5 ** 0.5)))


def _group_norm(t, gain, group):
    parts = []
    for s in range(0, t.shape[-1], group):
        p = t[:, s:s + group]
        parts.append(p * _rms_scale(p) * gain[:, s:s + group])
    return jnp.concatenate(parts, axis=-1)


def _ada_kernel(c_ref, w_ref, b_ref, o_ref):
    c = c_ref[...]
    cond = (c * jax.nn.sigmoid(c)).astype(BF16)
    o_ref[...] = jnp.dot(cond, w_ref[...].astype(BF16), preferred_element_type=F32) + b_ref[...]


def _ada_mod(c, w_ada, b_ada, tn=1024):
    b, d = c.shape
    n = w_ada.shape[1]
    return pl.pallas_call(
        _ada_kernel,
        grid=(n // tn,),
        in_specs=[
            pl.BlockSpec((b, d), lambda j: (0, 0)),
            pl.BlockSpec((d, tn), lambda j: (0, j)),
            pl.BlockSpec((1, tn), lambda j: (0, j)),
        ],
        out_specs=pl.BlockSpec((b, tn), lambda j: (0, j)),
        out_shape=jax.ShapeDtypeStruct((b, n), F32),
        compiler_params=pltpu.CompilerParams(
            dimension_semantics=("arbitrary",), vmem_limit_bytes=VMEM_LIMIT_BYTES),
        name="ada_mod",
    )(c, w_ada, b_ada.reshape(1, n))


def _in_proj_kernel(x_ref, mod_ref, g_ref, w_ref, sg_ref, *rest, sec, n_side):
    side_in, o_ref, side_out = rest[:n_side], rest[n_side], rest[n_side + 1:]
    for src, dst in zip(side_in, side_out):
        dst[...] = src[...].astype(BF16)

    x = x_ref[...]
    shift = mod_ref[0:1, :]
    scale = mod_ref[1:2, :]
    h = x * _rms_scale(x) * g_ref[...]
    h = (h * (1.0 + scale) + shift).astype(BF16)

    def section(s):
        return jnp.dot(h, w_ref[:, s * sec:(s + 1) * sec], preferred_element_type=F32)

    def put(s, val):
        o_ref[:, s * sec:(s + 1) * sec] = val.astype(BF16)

    put(0, _group_norm(section(0), sg_ref[0:1, :], HEAD_DIM))
    put(1, _group_norm(section(1), sg_ref[1:2, :], HEAD_DIM))
    put(2, section(2))
    put(3, _gelu(section(3)))
    put(4, _group_norm(_gelu(section(4)), sg_ref[2:3, :], HEAD_DIM))


def _in_proj(x2, mod3, g, w_in, sec_gain, seq, side_weights, tm=256):
    t, d = x2.shape
    n = w_in.shape[1]
    sec = sec_gain.shape[-1]
    steps = t // tm
    steps_per_batch = seq // tm
    side_specs = [pl.BlockSpec((w.shape[0] // steps, w.shape[1]), lambda i: (i, 0))
                  for w in side_weights]
    outs = pl.pallas_call(
        functools.partial(_in_proj_kernel, sec=sec, n_side=len(side_weights)),
        grid=(steps,),
        in_specs=[
            pl.BlockSpec((tm, d), lambda i: (i, 0)),
            pl.BlockSpec((None, N_MOD, d), lambda i: (i // steps_per_batch, 0, 0)),
            pl.BlockSpec((1, d), lambda i: (0, 0)),
            pl.BlockSpec((d, n), lambda i: (0, 0), pipeline_mode=pl.Buffered(1)),
            pl.BlockSpec(sec_gain.shape, lambda i: (0, 0)),
        ] + side_specs,
        out_specs=[pl.BlockSpec((tm, n), lambda i: (i, 0))] + side_specs,
        out_shape=[jax.ShapeDtypeStruct((t, n), BF16)]
        + [jax.ShapeDtypeStruct(w.shape, BF16) for w in side_weights],
        compiler_params=pltpu.CompilerParams(
            dimension_semantics=("arbitrary",), vmem_limit_bytes=VMEM_LIMIT_BYTES),
        name="in_proj",
    )(x2, mod3, g, w_in, sec_gain, *side_weights)
    return outs[0], outs[1:]


def _mixer_kernel(q_ref, k_ref, v_ref, base_ref, u_ref, z_ref, ws_ref, bs_ref,
                  o_ref, gm_ref, bias_ref, *, seq):
    _attention(q_ref, k_ref, v_ref, base_ref, o_ref, bias_ref, seq)
    _spatial_gate(u_ref, z_ref, ws_ref, bs_ref, gm_ref, seq)


def _spatial_gate(u_ref, z_ref, w_ref, b_ref, o_ref, seq):
    t = lax.broadcasted_iota(jnp.int32, (SPATIAL, SPATIAL), 0)
    s = lax.broadcasted_iota(jnp.int32, (SPATIAL, SPATIAL), 1)
    w = jnp.where((t // CHUNK) >= (s // CHUNK), w_ref[...], 0.0).astype(BF16)
    bias = b_ref[...]
    for n in range(seq // SPATIAL):
        rows = slice(n * SPATIAL, (n + 1) * SPATIAL)
        y = jnp.dot(w, z_ref[rows, :], preferred_element_type=F32) + bias
        o_ref[rows, :] = (u_ref[rows, :].astype(F32) * y).astype(BF16)


def _attention(q_ref, k_ref, v_ref, base_ref, o_ref, bias_ref, seq):
    scale = HEAD_DIM ** -0.5
    tab = pltpu.roll(jnp.broadcast_to(base_ref[...], (Q_BLOCK, ROLL_WIDTH)), 0, 1,
                     stride=1, stride_axis=0)[:, :KEY_WINDOW]
    r = lax.broadcasted_iota(jnp.int32, (Q_BLOCK, KEY_WINDOW), 0)
    col = lax.broadcasted_iota(jnp.int32, (Q_BLOCK, KEY_WINDOW), 1)
    qc = r // CHUNK
    kc = col // CHUNK - LEFT_CHUNKS
    bias_ref[...] = jnp.where((kc <= qc) & (kc >= qc - LEFT_CHUNKS), tab, NEG_INF)

    def window(qb):
        p0 = qb * Q_BLOCK
        ws = max(0, p0 - LEFT_KEYS)
        return p0, ws, ws - (p0 - LEFT_KEYS)

    def scores(qb):
        p0, ws, co = window(qb)
        s = lax.dot_general(q_ref[p0:p0 + Q_BLOCK, :], k_ref[ws:p0 + Q_BLOCK, :],
                            (((1,), (1,)), ((), ())), preferred_element_type=F32)
        return s * scale + bias_ref[:, co:KEY_WINDOW]

    def finish(qb, s):
        p0, ws, _ = window(qb)
        m = jnp.max(s, axis=-1, keepdims=True)
        p = jnp.exp(s - m)
        l = jnp.sum(p, axis=-1, keepdims=True)
        o = jnp.dot(p.astype(BF16), v_ref[ws:p0 + Q_BLOCK, :], preferred_element_type=F32)
        o_ref[p0:p0 + Q_BLOCK, :] = (o / l).astype(BF16)

    n_blocks = seq // Q_BLOCK
    pending = [scores(qb) for qb in range(SCORE_LOOKAHEAD)]
    for qb in range(n_blocks):
        if qb + SCORE_LOOKAHEAD < n_blocks:
            pending.append(scores(qb + SCORE_LOOKAHEAD))
        finish(qb, pending.pop(0))


def _bias_row(rel_bias):
    rb = rel_bias.astype(F32)
    h = rb.shape[0]
    far = rb[:, 2 * MAX_REL:]
    row = jnp.concatenate([
        jnp.broadcast_to(far, (h, LEFT_KEYS - MAX_REL + 1)),
        rb[:, 1:2 * MAX_REL][:, ::-1],
        jnp.broadcast_to(far, (h, ROLL_WIDTH - LEFT_KEYS - MAX_REL)),
    ], axis=1)
    return row.reshape(h, 1, ROLL_WIDTH)


def _mixers(proj3, bias_row, w_spatial, b_spatial):
    b, seq, _ = proj3.shape
    n = w_spatial.shape[0]
    blk = lambda sec: pl.BlockSpec((None, seq, HEAD_DIM), lambda bi, h: (bi, 0, sec * n + h))
    out_blk = pl.BlockSpec((None, seq, HEAD_DIM), lambda bi, h: (bi, 0, h))
    out = jax.ShapeDtypeStruct((b, seq, n * HEAD_DIM), BF16)
    return pl.pallas_call(
        functools.partial(_mixer_kernel, seq=seq),
        grid=(b, n),
        in_specs=[
            blk(0), blk(1), blk(2),
            pl.BlockSpec((None, 1, ROLL_WIDTH), lambda bi, h: (h, 0, 0)),
            blk(3), blk(4),
            pl.BlockSpec((None, SPATIAL, SPATIAL), lambda bi, h: (h, 0, 0)),
            pl.BlockSpec((None, SPATIAL, 1), lambda bi, h: (h, 0, 0)),
        ],
        out_specs=[out_blk, out_blk],
        out_shape=[out, out],
        scratch_shapes=[pltpu.VMEM((Q_BLOCK, KEY_WINDOW), F32)],
        compiler_params=pltpu.CompilerParams(
            dimension_semantics=("arbitrary", "arbitrary"), vmem_limit_bytes=VMEM_LIMIT_BYTES),
        name="mixers",
    )(proj3, proj3, proj3, bias_row, proj3, proj3, w_spatial,
      b_spatial.reshape(n, SPATIAL, 1))


def _out_proj_kernel(a_ref, m_ref, x_ref, mod_ref, ga_ref, gg_ref, gf_ref, w_ref, x1_ref, h2_ref,
                     *, row_chunk):
    gate_m = mod_ref[2:3, :]
    shift_f = mod_ref[3:4, :]
    scale_f = mod_ref[4:5, :]

    def mixed(r):
        rows = slice(r * row_chunk, (r + 1) * row_chunk)
        a = a_ref[rows, :].astype(F32)
        m = m_ref[rows, :].astype(F32)
        return jnp.concatenate(
            [(a * _rms_scale(a) * ga_ref[...]).astype(BF16),
             (m * _rms_scale(m) * gg_ref[...]).astype(BF16)], axis=-1)

    def project(mix):
        return jnp.dot(mix, w_ref[...], preferred_element_type=F32)

    def finish(r, y):
        rows = slice(r * row_chunk, (r + 1) * row_chunk)
        x1 = x_ref[rows, :] + gate_m * y
        x1_ref[rows, :] = x1
        h2 = x1 * _rms_scale(x1) * gf_ref[...]
        h2_ref[rows, :] = (h2 * (1.0 + scale_f) + shift_f).astype(BF16)

    n_chunks = a_ref.shape[0] // row_chunk
    mixes = [mixed(0)]
    ys = []
    for r in range(n_chunks + 1):
        if r + 1 < n_chunks:
            mixes.append(mixed(r + 1))
        if r < n_chunks:
            ys.append(project(mixes.pop(0)))
        if r >= 1:
            finish(r - 1, ys.pop(0))


def _out_proj(attn2, gm2, x2, mod3, ga, gg, gf, w_out, seq, tm=512, row_chunk=256):
    t, d = x2.shape
    wa = attn2.shape[1]
    wg = gm2.shape[1]
    steps_per_batch = seq // tm
    return pl.pallas_call(
        functools.partial(_out_proj_kernel, row_chunk=row_chunk),
        grid=(t // tm,),
        in_specs=[
            pl.BlockSpec((tm, wa), lambda i: (i, 0)),
            pl.BlockSpec((tm, wg), lambda i: (i, 0)),
            pl.BlockSpec((tm, d), lambda i: (i, 0)),
            pl.BlockSpec((None, N_MOD, d), lambda i: (i // steps_per_batch, 0, 0)),
            pl.BlockSpec((1, wa), lambda i: (0, 0)),
            pl.BlockSpec((1, wg), lambda i: (0, 0)),
            pl.BlockSpec((1, d), lambda i: (0, 0)),
            pl.BlockSpec((wa + wg, d), lambda i: (0, 0)),
        ],
        out_specs=[
            pl.BlockSpec((tm, d), lambda i: (i, 0)),
            pl.BlockSpec((tm, d), lambda i: (i, 0)),
        ],
        out_shape=[jax.ShapeDtypeStruct((t, d), F32), jax.ShapeDtypeStruct((t, d), BF16)],
        compiler_params=pltpu.CompilerParams(
            dimension_semantics=("arbitrary",), vmem_limit_bytes=VMEM_LIMIT_BYTES),
        name="out_proj",
    )(attn2, gm2, x2, mod3, ga, gg, gf, w_out)


def _ffn_kernel(h_ref, w1_ref, w2_ref, x1_ref, mod_ref, o_ref):
    j = pl.program_id(1)
    last = pl.num_programs(1) - 1

    def partial_out():
        a = jnp.dot(h_ref[...], w1_ref[...], preferred_element_type=F32)
        a = jnp.square(jnp.maximum(a, 0.0)).astype(BF16)
        return jnp.dot(a, w2_ref[...], preferred_element_type=F32)

    @pl.when(j == 0)
    def _():
        o_ref[...] = partial_out()

    @pl.when((j > 0) & (j < last))
    def _():
        o_ref[...] += partial_out()

    @pl.when(j == last)
    def _():
        gate_f = mod_ref[5:6, :]
        o_ref[...] = x1_ref[...] + gate_f * (o_ref[...] + partial_out())


def _ffn(h2, w1, w2, x1, mod3, seq, tm=512, tf=1024):
    t, d = h2.shape
    f = w1.shape[1]
    steps_per_batch = seq // tm
    assert f // tf >= 2, "first and last d_ff tiles must be distinct steps"
    return pl.pallas_call(
        _ffn_kernel,
        grid=(t // tm, f // tf),
        in_specs=[
            pl.BlockSpec((tm, d), lambda i, j: (i, 0)),
            pl.BlockSpec((d, tf), lambda i, j: (0, j)),
            pl.BlockSpec((tf, d), lambda i, j: (j, 0)),
            pl.BlockSpec((tm, d), lambda i, j: (i, 0)),
            pl.BlockSpec((None, N_MOD, d), lambda i, j: (i // steps_per_batch, 0, 0)),
        ],
        out_specs=pl.BlockSpec((tm, d), lambda i, j: (i, 0)),
        out_shape=jax.ShapeDtypeStruct((t, d), F32),
        compiler_params=pltpu.CompilerParams(
            dimension_semantics=("arbitrary", "arbitrary"), vmem_limit_bytes=VMEM_LIMIT_BYTES),
        name="ffn",
    )(h2, w1, w2, x1, mod3)


def kernel(x, c, w_ada, b_ada, mix_norm_g, w_in, q_norm_g, k_norm_g, rel_bias, gmlp_norm_g,
           w_spatial, b_spatial, attn_out_g, gmlp_out_g, w_out, ff_norm_g, w_ff1, w_ff2):
    b, seq, d = x.shape
    depth = w_ada.shape[0]
    n_heads = rel_bias.shape[1]
    n_groups = w_spatial.shape[1]
    attn_w = n_heads * HEAD_DIM
    gmlp_w = n_groups * HEAD_DIM
    assert n_heads == n_groups and w_in.shape[2] == 3 * attn_w + 2 * gmlp_w

    x2 = x.reshape(b * seq, d)
    for l in range(depth):
        mod3 = _ada_mod(c, w_ada[l], b_ada[l]).reshape(b, N_MOD, d)
        sec_gain = jnp.stack([
            jnp.tile(q_norm_g[l], n_heads), jnp.tile(k_norm_g[l], n_heads),
            gmlp_norm_g[l].reshape(-1)]).astype(F32)
        proj, (w_out_b, w_ff1_b, w_ff2_b) = _in_proj(
            x2, mod3, mix_norm_g[l].reshape(1, d), w_in[l].astype(BF16), sec_gain, seq,
            side_weights=(w_out[l], w_ff1[l], w_ff2[l]))
        proj3 = proj.reshape(b, seq, -1)
        attn, gm = _mixers(proj3, _bias_row(rel_bias[l]), w_spatial[l], b_spatial[l])
        x1, h2 = _out_proj(
            attn.reshape(b * seq, attn_w), gm.reshape(b * seq, gmlp_w), x2, mod3,
            attn_out_g[l].reshape(1, attn_w), gmlp_out_g[l].reshape(1, gmlp_w),
            ff_norm_g[l].reshape(1, d), w_out_b, seq)
        x2 = _ffn(h2, w_ff1_b, w_ff2_b, x1, mod3, seq)
    return x2.reshape(b, seq, d)
```

```python
import functools

import jax
import jax.numpy as jnp
from jax import lax
from jax.experimental import pallas as pl
from jax.experimental.pallas import tpu as pltpu

CHUNK = 64
LEFT_CHUNKS = 8
HEAD_DIM = 128
MAX_REL = 128
SPATIAL = 128
N_MOD = 6
EPS = 1e-6
NEG_INF = -1e30
LOG2_E = 1.4426950408889634

Q_BLOCK = 2 * CHUNK
KEY_WINDOW = (LEFT_CHUNKS + 2) * CHUNK
LEFT_KEYS = LEFT_CHUNKS * CHUNK
ROLL_WIDTH = 1024
SCORE_LOOKAHEAD = 2

VMEM_LIMIT_BYTES = 56 * 1024 * 1024

BF16 = jnp.bfloat16
F32 = jnp.float32


def _rms_scale(t):
    return lax.rsqrt(jnp.mean(t * t, axis=-1, keepdims=True) + EPS)


def _gelu(t):
    return 0.5 * t * (1.0 + lax.erf(t * (0.5 ** 0.5)))


def _group_norm(t, gain, group):
    parts = []
    for s in range(0, t.shape[-1], group):
        p = t[:, s:s + group]
        parts.append(p * _rms_scale(p) * gain[:, s:s + group])
    return jnp.concatenate(parts, axis=-1)


def _ada_kernel(c_ref, w_ref, b_ref, o_ref):
    c = c_ref[...]
    cond = (c * jax.nn.sigmoid(c)).astype(BF16)
    o_ref[...] = jnp.dot(cond, w_ref[...].astype(BF16), preferred_element_type=F32) + b_ref[...]


def _ada_mod(c, w_ada, b_ada, tn=1024):
    b, d = c.shape
    n = w_ada.shape[1]
    return pl.pallas_call(
        _ada_kernel,
        grid=(n // tn,),
        in_specs=[
            pl.BlockSpec((b, d), lambda j: (0, 0)),
            pl.BlockSpec((d, tn), lambda j: (0, j)),
            pl.BlockSpec((1, tn), lambda j: (0, j)),
        ],
        out_specs=pl.BlockSpec((b, tn), lambda j: (0, j)),
        out_shape=jax.ShapeDtypeStruct((b, n), F32),
        compiler_params=pltpu.CompilerParams(
            dimension_semantics=("arbitrary",), vmem_limit_bytes=VMEM_LIMIT_BYTES),
        name="ada_mod",
    )(c, w_ada, b_ada.reshape(1, n))


def _in_proj_kernel(x_ref, mod_ref, g_ref, w_ref, sg_ref, *rest, sec, n_side):
    side_in, o_ref, side_out = rest[:n_side], rest[n_side], rest[n_side + 1:]
    for src, dst in zip(side_in, side_out):
        dst[...] = src[...].astype(BF16)

    x = x_ref[...]
    shift = mod_ref[0:1, :]
    scale = mod_ref[1:2, :]
    h = x * _rms_scale(x) * g_ref[...]
    h = (h * (1.0 + scale) + shift).astype(BF16)

    def section(s):
        return jnp.dot(h, w_ref[:, s * sec:(s + 1) * sec], preferred_element_type=F32)

    def put(s, val):
        o_ref[:, s * sec:(s + 1) * sec] = val.astype(BF16)

    put(4, _group_norm(_gelu(section(4)), sg_ref[2:3, :], HEAD_DIM))
    put(3, _gelu(section(3)))
    put(0, _group_norm(section(0), sg_ref[0:1, :], HEAD_DIM))
    put(1, _group_norm(section(1), sg_ref[1:2, :], HEAD_DIM))
    put(2, section(2))


def _in_proj(x2, mod3, g, w_in, sec_gain, seq, side_weights, tm=256):
    t, d = x2.shape
    n = w_in.shape[1]
    sec = sec_gain.shape[-1]
    steps = t // tm
    steps_per_batch = seq // tm
    side_specs = [pl.BlockSpec((w.shape[0] // steps, w.shape[1]), lambda i: (i, 0))
                  for w in side_weights]
    outs = pl.pallas_call(
        functools.partial(_in_proj_kernel, sec=sec, n_side=len(side_weights)),
        grid=(steps,),
        in_specs=[
            pl.BlockSpec((tm, d), lambda i: (i, 0)),
            pl.BlockSpec((None, N_MOD, d), lambda i: (i // steps_per_batch, 0, 0)),
            pl.BlockSpec((1, d), lambda i: (0, 0)),
            pl.BlockSpec((d, n), lambda i: (0, 0), pipeline_mode=pl.Buffered(1)),
            pl.BlockSpec(sec_gain.shape, lambda i: (0, 0)),
        ] + side_specs,
        out_specs=[pl.BlockSpec((tm, n), lambda i: (i, 0))] + side_specs,
        out_shape=[jax.ShapeDtypeStruct((t, n), BF16)]
        + [jax.ShapeDtypeStruct(w.shape, BF16) for w in side_weights],
        compiler_params=pltpu.CompilerParams(
            dimension_semantics=("arbitrary",), vmem_limit_bytes=VMEM_LIMIT_BYTES),
        name="in_proj",
    )(x2, mod3, g, w_in, sec_gain, *side_weights)
    return outs[0], outs[1:]


def _mixer_kernel(q_ref, k_ref, v_ref, base_ref, u_ref, z_ref, ws_ref, bs_ref,
                  o_ref, gm_ref, bias_ref, vext_ref, *, seq):
    _attention(q_ref, k_ref, v_ref, base_ref, o_ref, bias_ref, vext_ref, seq)
    _spatial_gate(u_ref, z_ref, ws_ref, bs_ref, gm_ref, seq)


def _spatial_gate(u_ref, z_ref, w_ref, b_ref, o_ref, seq):
    t = lax.broadcasted_iota(jnp.int32, (SPATIAL, SPATIAL), 0)
    s = lax.broadcasted_iota(jnp.int32, (SPATIAL, SPATIAL), 1)
    w = jnp.where((t // CHUNK) >= (s // CHUNK), w_ref[...], 0.0).astype(BF16)
    bias = b_ref[...]
    for n in range(seq // SPATIAL):
        rows = slice(n * SPATIAL, (n + 1) * SPATIAL)
        y = jnp.dot(w, z_ref[rows, :], preferred_element_type=F32) + bias
        o_ref[rows, :] = (u_ref[rows, :].astype(F32) * y).astype(BF16)


def _attention(q_ref, k_ref, v_ref, base_ref, o_ref, bias_ref, vext_ref, seq):
    scale = HEAD_DIM ** -0.5 * LOG2_E
    tab = pltpu.roll(jnp.broadcast_to(base_ref[...], (Q_BLOCK, ROLL_WIDTH)), 0, 1,
                     stride=1, stride_axis=0)[:, :KEY_WINDOW]
    r = lax.broadcasted_iota(jnp.int32, (Q_BLOCK, KEY_WINDOW), 0)
    col = lax.broadcasted_iota(jnp.int32, (Q_BLOCK, KEY_WINDOW), 1)
    qc = r // CHUNK
    kc = col // CHUNK - LEFT_CHUNKS
    bias_ref[...] = jnp.where((kc <= qc) & (kc >= qc - LEFT_CHUNKS), tab * LOG2_E, NEG_INF)
    vext_ref[:, :HEAD_DIM] = v_ref[...]
    vext_ref[:, HEAD_DIM:] = jnp.ones((seq, HEAD_DIM), BF16)

    def window(qb):
        p0 = qb * Q_BLOCK
        ws = max(0, p0 - LEFT_KEYS)
        return p0, ws, ws - (p0 - LEFT_KEYS)

    def scores(qb):
        p0, ws, co = window(qb)
        s = lax.dot_general(q_ref[p0:p0 + Q_BLOCK, :], k_ref[ws:p0 + Q_BLOCK, :],
                            (((1,), (1,)), ((), ())), preferred_element_type=F32)
        return s * scale + bias_ref[:, co:KEY_WINDOW]

    def finish(qb, s):
        p0, ws, _ = window(qb)
        m = jnp.max(s, axis=-1, keepdims=True)
        p = jnp.exp2(s - m).astype(BF16)
        o = jnp.dot(p, vext_ref[ws:p0 + Q_BLOCK, :], preferred_element_type=F32)
        o_ref[p0:p0 + Q_BLOCK, :] = (o[:, :HEAD_DIM] / o[:, HEAD_DIM:]).astype(BF16)

    n_blocks = seq // Q_BLOCK
    pending = [scores(qb) for qb in range(SCORE_LOOKAHEAD)]
    for qb in range(n_blocks):
        if qb + SCORE_LOOKAHEAD < n_blocks:
            pending.append(scores(qb + SCORE_LOOKAHEAD))
        finish(qb, pending.pop(0))


def _bias_row(rel_bias):
    rb = rel_bias.astype(F32)
    h = rb.shape[0]
    far = rb[:, 2 * MAX_REL:]
    row = jnp.concatenate([
        jnp.broadcast_to(far, (h, LEFT_KEYS - MAX_REL + 1)),
        rb[:, 1:2 * MAX_REL][:, ::-1],
        jnp.broadcast_to(far, (h, ROLL_WIDTH - LEFT_KEYS - MAX_REL)),
    ], axis=1)
    return row.reshape(h, 1, ROLL_WIDTH)


def _mixers(proj3, bias_row, w_spatial, b_spatial):
    b, seq, _ = proj3.shape
    n = w_spatial.shape[0]
    blk = lambda sec: pl.BlockSpec((None, seq, HEAD_DIM), lambda bi, h: (bi, 0, sec * n + h))
    out_blk = pl.BlockSpec((None, seq, HEAD_DIM), lambda bi, h: (bi, 0, h))
    out = jax.ShapeDtypeStruct((b, seq, n * HEAD_DIM), BF16)
    return pl.pallas_call(
        functools.partial(_mixer_kernel, seq=seq),
        grid=(b, n),
        in_specs=[
            blk(0), blk(1), blk(2),
            pl.BlockSpec((None, 1, ROLL_WIDTH), lambda bi, h: (h, 0, 0)),
            blk(3), blk(4),
            pl.BlockSpec((None, SPATIAL, SPATIAL), lambda bi, h: (h, 0, 0)),
            pl.BlockSpec((None, SPATIAL, 1), lambda bi, h: (h, 0, 0)),
        ],
        out_specs=[out_blk, out_blk],
        out_shape=[out, out],
        scratch_shapes=[pltpu.VMEM((Q_BLOCK, KEY_WINDOW), F32),
                        pltpu.VMEM((seq, 2 * HEAD_DIM), BF16)],
        compiler_params=pltpu.CompilerParams(
            dimension_semantics=("arbitrary", "arbitrary"), vmem_limit_bytes=VMEM_LIMIT_BYTES),
        name="mixers",
    )(proj3, proj3, proj3, bias_row, proj3, proj3, w_spatial,
      b_spatial.reshape(n, SPATIAL, 1))


def _out_proj_kernel(a_ref, m_ref, x_ref, mod_ref, ga_ref, gg_ref, gf_ref, w_ref, x1_ref, h2_ref,
                     *, row_chunk):
    gate_m = mod_ref[2:3, :]
    shift_f = mod_ref[3:4, :]
    scale_f = mod_ref[4:5, :]

    def mixed(r):
        rows = slice(r * row_chunk, (r + 1) * row_chunk)
        a = a_ref[rows, :].astype(F32)
        m = m_ref[rows, :].astype(F32)
        return jnp.concatenate(
            [(a * _rms_scale(a) * ga_ref[...]).astype(BF16),
             (m * _rms_scale(m) * gg_ref[...]).astype(BF16)], axis=-1)

    def project(mix):
        return jnp.dot(mix, w_ref[...], preferred_element_type=F32)

    def finish(r, y):
        rows = slice(r * row_chunk, (r + 1) * row_chunk)
        x1 = x_ref[rows, :] + gate_m * y
        x1_ref[rows, :] = x1
        h2 = x1 * _rms_scale(x1) * gf_ref[...]
        h2_ref[rows, :] = (h2 * (1.0 + scale_f) + shift_f).astype(BF16)

    n_chunks = a_ref.shape[0] // row_chunk
    mixes = [mixed(0)]
    ys = []
    for r in range(n_chunks + 1):
        if r + 1 < n_chunks:
            mixes.append(mixed(r + 1))
        if r < n_chunks:
            ys.append(project(mixes.pop(0)))
        if r >= 1:
            finish(r - 1, ys.pop(0))


def _out_proj(attn2, gm2, x2, mod3, ga, gg, gf, w_out, seq, tm=512, row_chunk=256):
    t, d = x2.shape
    wa = attn2.shape[1]
    wg = gm2.shape[1]
    steps_per_batch = seq // tm
    return pl.pallas_call(
        functools.partial(_out_proj_kernel, row_chunk=row_chunk),
        grid=(t // tm,),
        in_specs=[
            pl.BlockSpec((tm, wa), lambda i: (i, 0)),
            pl.BlockSpec((tm, wg), lambda i: (i, 0)),
            pl.BlockSpec((tm, d), lambda i: (i, 0)),
            pl.BlockSpec((None, N_MOD, d), lambda i: (i // steps_per_batch, 0, 0)),
            pl.BlockSpec((1, wa), lambda i: (0, 0)),
            pl.BlockSpec((1, wg), lambda i: (0, 0)),
            pl.BlockSpec((1, d), lambda i: (0, 0)),
            pl.BlockSpec((wa + wg, d), lambda i: (0, 0)),
        ],
        out_specs=[
            pl.BlockSpec((tm, d), lambda i: (i, 0)),
            pl.BlockSpec((tm, d), lambda i: (i, 0)),
        ],
        out_shape=[jax.ShapeDtypeStruct((t, d), F32), jax.ShapeDtypeStruct((t, d), BF16)],
        compiler_params=pltpu.CompilerParams(
            dimension_semantics=("arbitrary",), vmem_limit_bytes=VMEM_LIMIT_BYTES),
        name="out_proj",
    )(attn2, gm2, x2, mod3, ga, gg, gf, w_out)


def _ffn_kernel(h_ref, w1_ref, w2_ref, x1_ref, mod_ref, o_ref):
    j = pl.program_id(1)
    last = pl.num_programs(1) - 1

    def partial_out():
        a = jnp.dot(h_ref[...], w1_ref[...], preferred_element_type=F32)
        a = jnp.square(jnp.maximum(a, 0.0)).astype(BF16)
        return jnp.dot(a, w2_ref[...], preferred_element_type=F32)

    @pl.when(j == 0)
    def _():
        o_ref[...] = partial_out()

    @pl.when((j > 0) & (j < last))
    def _():
        o_ref[...] += partial_out()

    @pl.when(j == last)
    def _():
        gate_f = mod_ref[5:6, :]
        o_ref[...] = x1_ref[...] + gate_f * (o_ref[...] + partial_out())


def _ffn(h2, w1, w2, x1, mod3, seq, tm=512, tf=1024):
    t, d = h2.shape
    f = w1.shape[1]
    steps_per_batch = seq // tm
    assert f // tf >= 2, "first and last d_ff tiles must be distinct steps"
    return pl.pallas_call(
        _ffn_kernel,
        grid=(t // tm, f // tf),
        in_specs=[
            pl.BlockSpec((tm, d), lambda i, j: (i, 0)),
            pl.BlockSpec((d, tf), lambda i, j: (0, j)),
            pl.BlockSpec((tf, d), lambda i, j: (j, 0)),
            pl.BlockSpec((tm, d), lambda i, j: (i, 0)),
            pl.BlockSpec((None, N_MOD, d), lambda i, j: (i // steps_per_batch, 0, 0)),
        ],
        out_specs=pl.BlockSpec((tm, d), lambda i, j: (i, 0)),
        out_shape=jax.ShapeDtypeStruct((t, d), F32),
        compiler_params=pltpu.CompilerParams(
            dimension_semantics=("arbitrary", "arbitrary"), vmem_limit_bytes=VMEM_LIMIT_BYTES),
        name="ffn",
    )(h2, w1, w2, x1, mod3)


def kernel(x, c, w_ada, b_ada, mix_norm_g, w_in, q_norm_g, k_norm_g, rel_bias, gmlp_norm_g,
           w_spatial, b_spatial, attn_out_g, gmlp_out_g, w_out, ff_norm_g, w_ff1, w_ff2):
    b, seq, d = x.shape
    depth = w_ada.shape[0]
    n_heads = rel_bias.shape[1]
    n_groups = w_spatial.shape[1]
    attn_w = n_heads * HEAD_DIM
    gmlp_w = n_groups * HEAD_DIM
    assert n_heads == n_groups and w_in.shape[2] == 3 * attn_w + 2 * gmlp_w

    x2 = x.reshape(b * seq, d)
    for l in range(depth):
        mod3 = _ada_mod(c, w_ada[l], b_ada[l]).reshape(b, N_MOD, d)
        sec_gain = jnp.stack([
            jnp.tile(q_norm_g[l], n_heads), jnp.tile(k_norm_g[l], n_heads),
            gmlp_norm_g[l].reshape(-1)]).astype(F32)
        proj, (w_out_b, w_ff1_b, w_ff2_b) = _in_proj(
            x2, mod3, mix_norm_g[l].reshape(1, d), w_in[l].astype(BF16), sec_gain, seq,
            side_weights=(w_out[l], w_ff1[l], w_ff2[l]))
        proj3 = proj.reshape(b, seq, -1)
        attn, gm = _mixers(proj3, _bias_row(rel_bias[l]), w_spatial[l], b_spatial[l])
        x1, h2 = _out_proj(
            attn.reshape(b * seq, attn_w), gm.reshape(b * seq, gmlp_w), x2, mod3,
            attn_out_g[l].reshape(1, attn_w), gmlp_out_g[l].reshape(1, gmlp_w),
            ff_norm_g[l].reshape(1, d), w_out_b, seq)
        x2 = _ffn(h2, w_ff1_b, w_ff2_b, x1, mod3, seq)
    return x2.reshape(b, seq, d)
```

```python
import functools

import jax
import jax.numpy as jnp
from jax import lax
from jax.experimental import pallas as pl
from jax.experimental.pallas import tpu as pltpu

CHUNK = 64
LEFT_CHUNKS = 8
HEAD_DIM = 128
MAX_REL = 128
SPATIAL = 128
N_MOD = 6
EPS = 1e-6
NEG_INF = -1e30
LOG2_E = 1.4426950408889634

Q_BLOCK = 2 * CHUNK
KEY_WINDOW = (LEFT_CHUNKS + 2) * CHUNK
LEFT_KEYS = LEFT_CHUNKS * CHUNK
ROLL_WIDTH = 1024
SCORE_LOOKAHEAD = 2

VMEM_LIMIT_BYTES = 56 * 1024 * 1024

BF16 = jnp.bfloat16
F32 = jnp.float32


def _rms_scale(t):
    return lax.rsqrt(jnp.mean(t * t, axis=-1, keepdims=True) + EPS)


def _gelu(t):
    return 0.5 * t * (1.0 + lax.erf(t * (0.5 ** 0.5)))


def _group_norm(t, gain, group):
    parts = []
    for s in range(0, t.shape[-1], group):
        p = t[:, s:s + group]
        parts.append(p * _rms_scale(p) * gain[:, s:s + group])
    return jnp.concatenate(parts, axis=-1)


def _ada_kernel(c_ref, w_ref, b_ref, o_ref):
    c = c_ref[...]
    cond = (c * jax.nn.sigmoid(c)).astype(BF16)
    o_ref[...] = jnp.dot(cond, w_ref[...].astype(BF16), preferred_element_type=F32) + b_ref[...]


def _ada_mod(c, w_ada, b_ada, tn=1024):
    b, d = c.shape
    n = w_ada.shape[1]
    return pl.pallas_call(
        _ada_kernel,
        grid=(n // tn,),
        in_specs=[
            pl.BlockSpec((b, d), lambda j: (0, 0)),
            pl.BlockSpec((d, tn), lambda j: (0, j)),
            pl.BlockSpec((1, tn), lambda j: (0, j)),
        ],
        out_specs=pl.BlockSpec((b, tn), lambda j: (0, j)),
        out_shape=jax.ShapeDtypeStruct((b, n), F32),
        compiler_params=pltpu.CompilerParams(
            dimension_semantics=("arbitrary",), vmem_limit_bytes=VMEM_LIMIT_BYTES),
        name="ada_mod",
    )(c, w_ada, b_ada.reshape(1, n))


def _in_proj_kernel(x_ref, mod_ref, g_ref, w_ref, sg_ref, *rest, sec, n_side):
    side_in, o_ref, side_out = rest[:n_side], rest[n_side], rest[n_side + 1:]
    for src, dst in zip(side_in, side_out):
        dst[...] = src[...].astype(BF16)

    x = x_ref[...]
    shift = mod_ref[0:1, :]
    scale = mod_ref[1:2, :]
    h = x * _rms_scale(x) * g_ref[...]
    h = (h * (1.0 + scale) + shift).astype(BF16)

    def section(s):
        return jnp.dot(h, w_ref[:, s * sec:(s + 1) * sec], preferred_element_type=F32)

    def put(s, val):
        o_ref[:, s * sec:(s + 1) * sec] = val.astype(BF16)

    put(4, _group_norm(_gelu(section(4)), sg_ref[2:3, :], HEAD_DIM))
    put(3, _gelu(section(3)))
    put(0, _group_norm(section(0), sg_ref[0:1, :], HEAD_DIM))
    put(1, _group_norm(section(1), sg_ref[1:2, :], HEAD_DIM))
    put(2, section(2))


def _in_proj(x2, mod3, g, w_in, sec_gain, seq, side_weights, tm=512):
    t, d = x2.shape
    n = w_in.shape[1]
    sec = sec_gain.shape[-1]
    steps = t // tm
    steps_per_batch = seq // tm
    side_specs = [pl.BlockSpec((w.shape[0] // steps, w.shape[1]), lambda i: (i, 0))
                  for w in side_weights]
    outs = pl.pallas_call(
        functools.partial(_in_proj_kernel, sec=sec, n_side=len(side_weights)),
        grid=(steps,),
        in_specs=[
            pl.BlockSpec((tm, d), lambda i: (i, 0)),
            pl.BlockSpec((None, N_MOD, d), lambda i: (i // steps_per_batch, 0, 0)),
            pl.BlockSpec((1, d), lambda i: (0, 0)),
            pl.BlockSpec((d, n), lambda i: (0, 0), pipeline_mode=pl.Buffered(1)),
            pl.BlockSpec(sec_gain.shape, lambda i: (0, 0)),
        ] + side_specs,
        out_specs=[pl.BlockSpec((tm, n), lambda i: (i, 0))] + side_specs,
        out_shape=[jax.ShapeDtypeStruct((t, n), BF16)]
        + [jax.ShapeDtypeStruct(w.shape, BF16) for w in side_weights],
        compiler_params=pltpu.CompilerParams(
            dimension_semantics=("arbitrary",), vmem_limit_bytes=VMEM_LIMIT_BYTES),
        name="in_proj",
    )(x2, mod3, g, w_in, sec_gain, *side_weights)
    return outs[0], outs[1:]


def _mixer_kernel(q_ref, k_ref, v_ref, base_ref, u_ref, z_ref, ws_ref, bs_ref,
                  o_ref, gm_ref, bias_ref, vext_ref, *, seq):
    _attention(q_ref, k_ref, v_ref, base_ref, o_ref, bias_ref, vext_ref, seq)
    _spatial_gate(u_ref, z_ref, ws_ref, bs_ref, gm_ref, seq)


def _spatial_gate(u_ref, z_ref, w_ref, b_ref, o_ref, seq):
    t = lax.broadcasted_iota(jnp.int32, (SPATIAL, SPATIAL), 0)
    s = lax.broadcasted_iota(jnp.int32, (SPATIAL, SPATIAL), 1)
    w = jnp.where((t // CHUNK) >= (s // CHUNK), w_ref[...], 0.0).astype(BF16)
    bias = b_ref[...]
    for n in range(seq // SPATIAL):
        rows = slice(n * SPATIAL, (n + 1) * SPATIAL)
        y = jnp.dot(w, z_ref[rows, :], preferred_element_type=F32) + bias
        o_ref[rows, :] = (u_ref[rows, :].astype(F32) * y).astype(BF16)


def _attention(q_ref, k_ref, v_ref, base_ref, o_ref, bias_ref, vext_ref, seq):
    scale = HEAD_DIM ** -0.5 * LOG2_E
    tab = pltpu.roll(jnp.broadcast_to(base_ref[...], (Q_BLOCK, ROLL_WIDTH)), 0, 1,
                     stride=1, stride_axis=0)[:, :KEY_WINDOW]
    r = lax.broadcasted_iota(jnp.int32, (Q_BLOCK, KEY_WINDOW), 0)
    col = lax.broadcasted_iota(jnp.int32, (Q_BLOCK, KEY_WINDOW), 1)
    qc = r // CHUNK
    kc = col // CHUNK - LEFT_CHUNKS
    bias_ref[...] = jnp.where((kc <= qc) & (kc >= qc - LEFT_CHUNKS), tab * LOG2_E, NEG_INF)
    vext_ref[:, :HEAD_DIM] = v_ref[...]
    vext_ref[:, HEAD_DIM:] = jnp.ones((seq, HEAD_DIM), BF16)

    def window(qb):
        p0 = qb * Q_BLOCK
        ws = max(0, p0 - LEFT_KEYS)
        return p0, ws, ws - (p0 - LEFT_KEYS)

    def scores(qb):
        p0, ws, co = window(qb)
        s = lax.dot_general(q_ref[p0:p0 + Q_BLOCK, :], k_ref[ws:p0 + Q_BLOCK, :],
                            (((1,), (1,)), ((), ())), preferred_element_type=F32)
        return s * scale + bias_ref[:, co:KEY_WINDOW]

    def finish(qb, s):
        p0, ws, _ = window(qb)
        m = jnp.max(s, axis=-1, keepdims=True)
        p = jnp.exp2(s - m).astype(BF16)
        o = jnp.dot(p, vext_ref[ws:p0 + Q_BLOCK, :], preferred_element_type=F32)
        o_ref[p0:p0 + Q_BLOCK, :] = (o[:, :HEAD_DIM] / o[:, HEAD_DIM:]).astype(BF16)

    n_blocks = seq // Q_BLOCK
    pending = [scores(qb) for qb in range(SCORE_LOOKAHEAD)]
    for qb in range(n_blocks):
        if qb + SCORE_LOOKAHEAD < n_blocks:
            pending.append(scores(qb + SCORE_LOOKAHEAD))
        finish(qb, pending.pop(0))


def _bias_row(rel_bias):
    rb = rel_bias.astype(F32)
    h = rb.shape[0]
    far = rb[:, 2 * MAX_REL:]
    row = jnp.concatenate([
        jnp.broadcast_to(far, (h, LEFT_KEYS - MAX_REL + 1)),
        rb[:, 1:2 * MAX_REL][:, ::-1],
        jnp.broadcast_to(far, (h, ROLL_WIDTH - LEFT_KEYS - MAX_REL)),
    ], axis=1)
    return row.reshape(h, 1, ROLL_WIDTH)


def _mixers(proj3, bias_row, w_spatial, b_spatial):
    b, seq, _ = proj3.shape
    n = w_spatial.shape[0]
    blk = lambda sec: pl.BlockSpec((None, seq, HEAD_DIM), lambda bi, h: (bi, 0, sec * n + h))
    out_blk = pl.BlockSpec((None, seq, HEAD_DIM), lambda bi, h: (bi, 0, h))
    out = jax.ShapeDtypeStruct((b, seq, n * HEAD_DIM), BF16)
    return pl.pallas_call(
        functools.partial(_mixer_kernel, seq=seq),
        grid=(b, n),
        in_specs=[
            blk(0), blk(1), blk(2),
            pl.BlockSpec((None, 1, ROLL_WIDTH), lambda bi, h: (h, 0, 0)),
            blk(3), blk(4),
            pl.BlockSpec((None, SPATIAL, SPATIAL), lambda bi, h: (h, 0, 0)),
            pl.BlockSpec((None, SPATIAL, 1), lambda bi, h: (h, 0, 0)),
        ],
        out_specs=[out_blk, out_blk],
        out_shape=[out, out],
        scratch_shapes=[pltpu.VMEM((Q_BLOCK, KEY_WINDOW), F32),
                        pltpu.VMEM((seq, 2 * HEAD_DIM), BF16)],
        compiler_params=pltpu.CompilerParams(
            dimension_semantics=("arbitrary", "arbitrary"), vmem_limit_bytes=VMEM_LIMIT_BYTES),
        name="mixers",
    )(proj3, proj3, proj3, bias_row, proj3, proj3, w_spatial,
      b_spatial.reshape(n, SPATIAL, 1))


def _out_proj_kernel(a_ref, m_ref, x_ref, mod_ref, ga_ref, gg_ref, gf_ref, w_ref, x1_ref, h2_ref,
                     *, row_chunk):
    gate_m = mod_ref[2:3, :]
    shift_f = mod_ref[3:4, :]
    scale_f = mod_ref[4:5, :]

    def mixed(r):
        rows = slice(r * row_chunk, (r + 1) * row_chunk)
        a = a_ref[rows, :].astype(F32)
        m = m_ref[rows, :].astype(F32)
        return jnp.concatenate(
            [(a * _rms_scale(a) * ga_ref[...]).astype(BF16),
             (m * _rms_scale(m) * gg_ref[...]).astype(BF16)], axis=-1)

    def project(mix):
        return jnp.dot(mix, w_ref[...], preferred_element_type=F32)

    def finish(r, y):
        rows = slice(r * row_chunk, (r + 1) * row_chunk)
        x1 = x_ref[rows, :] + gate_m * y
        x1_ref[rows, :] = x1
        h2 = x1 * _rms_scale(x1) * gf_ref[...]
        h2_ref[rows, :] = (h2 * (1.0 + scale_f) + shift_f).astype(BF16)

    n_chunks = a_ref.shape[0] // row_chunk
    mixes = [mixed(0)]
    ys = []
    for r in range(n_chunks + 1):
        if r + 1 < n_chunks:
            mixes.append(mixed(r + 1))
        if r < n_chunks:
            ys.append(project(mixes.pop(0)))
        if r >= 1:
            finish(r - 1, ys.pop(0))


def _out_proj(attn2, gm2, x2, mod3, ga, gg, gf, w_out, seq, tm=512, row_chunk=256):
    t, d = x2.shape
    wa = attn2.shape[1]
    wg = gm2.shape[1]
    steps_per_batch = seq // tm
    return pl.pallas_call(
        functools.partial(_out_proj_kernel, row_chunk=row_chunk),
        grid=(t // tm,),
        in_specs=[
            pl.BlockSpec((tm, wa), lambda i: (i, 0)),
            pl.BlockSpec((tm, wg), lambda i: (i, 0)),
            pl.BlockSpec((tm, d), lambda i: (i, 0)),
            pl.BlockSpec((None, N_MOD, d), lambda i: (i // steps_per_batch, 0, 0)),
            pl.BlockSpec((1, wa), lambda i: (0, 0)),
            pl.BlockSpec((1, wg), lambda i: (0, 0)),
            pl.BlockSpec((1, d), lambda i: (0, 0)),
            pl.BlockSpec((wa + wg, d), lambda i: (0, 0)),
        ],
        out_specs=[
            pl.BlockSpec((tm, d), lambda i: (i, 0)),
            pl.BlockSpec((tm, d), lambda i: (i, 0)),
        ],
        out_shape=[jax.ShapeDtypeStruct((t, d), F32), jax.ShapeDtypeStruct((t, d), BF16)],
        compiler_params=pltpu.CompilerParams(
            dimension_semantics=("arbitrary",), vmem_limit_bytes=VMEM_LIMIT_BYTES),
        name="out_proj",
    )(attn2, gm2, x2, mod3, ga, gg, gf, w_out)


def _ffn_kernel(h_ref, w1_ref, w2_ref, x1_ref, mod_ref, o_ref):
    j = pl.program_id(1)
    last = pl.num_programs(1) - 1

    def partial_out():
        a = jnp.dot(h_ref[...], w1_ref[...], preferred_element_type=F32)
        a = jnp.square(jnp.maximum(a, 0.0)).astype(BF16)
        return jnp.dot(a, w2_ref[...], preferred_element_type=F32)

    @pl.when(j == 0)
    def _():
        o_ref[...] = partial_out()

    @pl.when((j > 0) & (j < last))
    def _():
        o_ref[...] += partial_out()

    @pl.when(j == last)
    def _():
        gate_f = mod_ref[5:6, :]
        o_ref[...] = x1_ref[...] + gate_f * (o_ref[...] + partial_out())


def _ffn(h2, w1, w2, x1, mod3, seq, tm=1024, tf=512):
    t, d = h2.shape
    f = w1.shape[1]
    steps_per_batch = seq // tm
    assert f // tf >= 2, "first and last d_ff tiles must be distinct steps"
    return pl.pallas_call(
        _ffn_kernel,
        grid=(t // tm, f // tf),
        in_specs=[
            pl.BlockSpec((tm, d), lambda i, j: (i, 0)),
            pl.BlockSpec((d, tf), lambda i, j: (0, j)),
            pl.BlockSpec((tf, d), lambda i, j: (j, 0)),
            pl.BlockSpec((tm, d), lambda i, j: (i, 0)),
            pl.BlockSpec((None, N_MOD, d), lambda i, j: (i // steps_per_batch, 0, 0)),
        ],
        out_specs=pl.BlockSpec((tm, d), lambda i, j: (i, 0)),
        out_shape=jax.ShapeDtypeStruct((t, d), F32),
        compiler_params=pltpu.CompilerParams(
            dimension_semantics=("arbitrary", "arbitrary"), vmem_limit_bytes=VMEM_LIMIT_BYTES),
        name="ffn",
    )(h2, w1, w2, x1, mod3)


def kernel(x, c, w_ada, b_ada, mix_norm_g, w_in, q_norm_g, k_norm_g, rel_bias, gmlp_norm_g,
           w_spatial, b_spatial, attn_out_g, gmlp_out_g, w_out, ff_norm_g, w_ff1, w_ff2):
    b, seq, d = x.shape
    depth = w_ada.shape[0]
    n_heads = rel_bias.shape[1]
    n_groups = w_spatial.shape[1]
    attn_w = n_heads * HEAD_DIM
    gmlp_w = n_groups * HEAD_DIM
    assert n_heads == n_groups and w_in.shape[2] == 3 * attn_w + 2 * gmlp_w

    x2 = x.reshape(b * seq, d)
    for l in range(depth):
        mod3 = _ada_mod(c, w_ada[l], b_ada[l]).reshape(b, N_MOD, d)
        sec_gain = jnp.stack([
            jnp.tile(q_norm_g[l], n_heads), jnp.tile(k_norm_g[l], n_heads),
            gmlp_norm_g[l].reshape(-1)]).astype(F32)
        proj, (w_out_b, w_ff1_b, w_ff2_b) = _in_proj(
            x2, mod3, mix_norm_g[l].reshape(1, d), w_in[l].astype(BF16), sec_gain, seq,
            side_weights=(w_out[l], w_ff1[l], w_ff2[l]))
        proj3 = proj.reshape(b, seq, -1)
        attn, gm = _mixers(proj3, _bias_row(rel_bias[l]), w_spatial[l], b_spatial[l])
        x1, h2 = _out_proj(
            attn.reshape(b * seq, attn_w), gm.reshape(b * seq, gmlp_w), x2, mod3,
            attn_out_g[l].reshape(1, attn_w), gmlp_out_g[l].reshape(1, gmlp_w),
            ff_norm_g[l].reshape(1, d), w_out_b, seq)
        x2 = _ffn(h2, w_ff1_b, w_ff2_b, x1, mod3, seq)
    return x2.reshape(b, seq, d)
```

```python
import functools

import jax
import jax.numpy as jnp
from jax import lax
from jax.experimental import pallas as pl
from jax.experimental.pallas import tpu as pltpu

CHUNK = 64
LEFT_CHUNKS = 8
HEAD_DIM = 128
MAX_REL = 128
SPATIAL = 128
N_MOD = 6
EPS = 1e-6
NEG_INF = -1e30
LOG2_E = 1.4426950408889634

Q_BLOCK = 2 * CHUNK
KEY_WINDOW = (LEFT_CHUNKS + 2) * CHUNK
LEFT_KEYS = LEFT_CHUNKS * CHUNK
ROLL_WIDTH = 1024
SCORE_LOOKAHEAD = 2

VMEM_LIMIT_BYTES = 56 * 1024 * 1024

BF16 = jnp.bfloat16
F32 = jnp.float32


def _rms_scale(t):
    return lax.rsqrt(jnp.mean(t * t, axis=-1, keepdims=True) + EPS)


def _gelu(t):
    return 0.5 * t * (1.0 + lax.erf(t * (0.5 ** 0.5)))


def _group_norm(t, gain, group):
    parts = []
    for s in range(0, t.shape[-1], group):
        p = t[:, s:s + group]
        parts.append(p * _rms_scale(p) * gain[:, s:s + group])
    return jnp.concatenate(parts, axis=-1)


def _ada_kernel(c_ref, w_ref, b_ref, o_ref):
    c = c_ref[...]
    cond = (c * jax.nn.sigmoid(c)).astype(BF16)
    o_ref[...] = jnp.dot(cond, w_ref[...].astype(BF16), preferred_element_type=F32) + b_ref[...]


def _ada_mod(c, w_ada, b_ada, tn=1024):
    b, d = c.shape
    n = w_ada.shape[1]
    return pl.pallas_call(
        _ada_kernel,
        grid=(n // tn,),
        in_specs=[
            pl.BlockSpec((b, d), lambda j: (0, 0)),
            pl.BlockSpec((d, tn), lambda j: (0, j)),
            pl.BlockSpec((1, tn), lambda j: (0, j)),
        ],
        out_specs=pl.BlockSpec((b, tn), lambda j: (0, j)),
        out_shape=jax.ShapeDtypeStruct((b, n), F32),
        compiler_params=pltpu.CompilerParams(
            dimension_semantics=("arbitrary",), vmem_limit_bytes=VMEM_LIMIT_BYTES),
        name="ada_mod",
    )(c, w_ada, b_ada.reshape(1, n))


def _in_proj_kernel(x_ref, mod_ref, g_ref, w_ref, sg_ref, *rest, sec, n_side):
    side_in, o_ref, side_out = rest[:n_side], rest[n_side], rest[n_side + 1:]
    for src, dst in zip(side_in, side_out):
        dst[...] = src[...].astype(BF16)

    x = x_ref[...]
    shift = mod_ref[0:1, :]
    scale = mod_ref[1:2, :]
    h = x * _rms_scale(x) * g_ref[...]
    h = (h * (1.0 + scale) + shift).astype(BF16)

    def section(s):
        return jnp.dot(h, w_ref[:, s * sec:(s + 1) * sec], preferred_element_type=F32)

    def put(s, val):
        val = val.astype(BF16)
        groups = sec // HEAD_DIM
        for g in range(groups):
            o_ref[s * groups + g] = val[:, g * HEAD_DIM:(g + 1) * HEAD_DIM]

    put(4, _group_norm(_gelu(section(4)), sg_ref[2:3, :], HEAD_DIM))
    put(3, _gelu(section(3)))
    put(0, _group_norm(section(0), sg_ref[0:1, :], HEAD_DIM))
    put(1, _group_norm(section(1), sg_ref[1:2, :], HEAD_DIM))
    put(2, section(2))


def _in_proj(x2, mod3, g, w_in, sec_gain, seq, side_weights, tm=256):
    t, d = x2.shape
    n = w_in.shape[1]
    sec = sec_gain.shape[-1]
    slabs = n // HEAD_DIM
    steps = t // tm
    steps_per_batch = seq // tm
    side_specs = [pl.BlockSpec((w.shape[0] // steps, w.shape[1]), lambda i: (i, 0))
                  for w in side_weights]
    outs = pl.pallas_call(
        functools.partial(_in_proj_kernel, sec=sec, n_side=len(side_weights)),
        grid=(steps,),
        in_specs=[
            pl.BlockSpec((tm, d), lambda i: (i, 0)),
            pl.BlockSpec((None, N_MOD, d), lambda i: (i // steps_per_batch, 0, 0)),
            pl.BlockSpec((1, d), lambda i: (0, 0)),
            pl.BlockSpec((d, n), lambda i: (0, 0), pipeline_mode=pl.Buffered(1)),
            pl.BlockSpec(sec_gain.shape, lambda i: (0, 0)),
        ] + side_specs,
        out_specs=[pl.BlockSpec((slabs, tm, HEAD_DIM), lambda i: (0, i, 0))] + side_specs,
        out_shape=[jax.ShapeDtypeStruct((slabs, t, HEAD_DIM), BF16)]
        + [jax.ShapeDtypeStruct(w.shape, BF16) for w in side_weights],
        compiler_params=pltpu.CompilerParams(
            dimension_semantics=("arbitrary",), vmem_limit_bytes=VMEM_LIMIT_BYTES),
        name="in_proj",
    )(x2, mod3, g, w_in, sec_gain, *side_weights)
    return outs[0], outs[1:]


def _mixer_kernel(q_ref, k_ref, v_ref, base_ref, u_ref, z_ref, ws_ref, bs_ref,
                  o_ref, gm_ref, bias_ref, vext_ref, *, seq):
    _attention(q_ref, k_ref, v_ref, base_ref, o_ref, bias_ref, vext_ref, seq)
    _spatial_gate(u_ref, z_ref, ws_ref, bs_ref, gm_ref, seq)


def _spatial_gate(u_ref, z_ref, w_ref, b_ref, o_ref, seq):
    t = lax.broadcasted_iota(jnp.int32, (SPATIAL, SPATIAL), 0)
    s = lax.broadcasted_iota(jnp.int32, (SPATIAL, SPATIAL), 1)
    w = jnp.where((t // CHUNK) >= (s // CHUNK), w_ref[...], 0.0).astype(BF16)
    bias = b_ref[...]
    for n in range(seq // SPATIAL):
        rows = slice(n * SPATIAL, (n + 1) * SPATIAL)
        y = jnp.dot(w, z_ref[rows, :], preferred_element_type=F32) + bias
        o_ref[rows, :] = (u_ref[rows, :].astype(F32) * y).astype(BF16)


def _attention(q_ref, k_ref, v_ref, base_ref, o_ref, bias_ref, vext_ref, seq):
    scale = HEAD_DIM ** -0.5 * LOG2_E
    tab = pltpu.roll(jnp.broadcast_to(base_ref[...], (Q_BLOCK, ROLL_WIDTH)), 0, 1,
                     stride=1, stride_axis=0)[:, :KEY_WINDOW]
    r = lax.broadcasted_iota(jnp.int32, (Q_BLOCK, KEY_WINDOW), 0)
    col = lax.broadcasted_iota(jnp.int32, (Q_BLOCK, KEY_WINDOW), 1)
    qc = r // CHUNK
    kc = col // CHUNK - LEFT_CHUNKS
    bias_ref[...] = jnp.where((kc <= qc) & (kc >= qc - LEFT_CHUNKS), tab * LOG2_E, NEG_INF)
    vext_ref[:, :HEAD_DIM] = v_ref[...]
    vext_ref[:, HEAD_DIM:] = jnp.ones((seq, HEAD_DIM), BF16)

    def window(qb):
        p0 = qb * Q_BLOCK
        ws = max(0, p0 - LEFT_KEYS)
        return p0, ws, ws - (p0 - LEFT_KEYS)

    def scores(qb):
        p0, ws, co = window(qb)
        s = lax.dot_general(q_ref[p0:p0 + Q_BLOCK, :], k_ref[ws:p0 + Q_BLOCK, :],
                            (((1,), (1,)), ((), ())), preferred_element_type=F32)
        return s * scale + bias_ref[:, co:KEY_WINDOW]

    def finish(qb, s):
        p0, ws, _ = window(qb)
        m = jnp.max(s, axis=-1, keepdims=True)
        p = jnp.exp2(s - m).astype(BF16)
        o = jnp.dot(p, vext_ref[ws:p0 + Q_BLOCK, :], preferred_element_type=F32)
        o_ref[p0:p0 + Q_BLOCK, :] = (o[:, :HEAD_DIM] / o[:, HEAD_DIM:]).astype(BF16)

    n_blocks = seq // Q_BLOCK
    pending = [scores(qb) for qb in range(SCORE_LOOKAHEAD)]
    for qb in range(n_blocks):
        if qb + SCORE_LOOKAHEAD < n_blocks:
            pending.append(scores(qb + SCORE_LOOKAHEAD))
        finish(qb, pending.pop(0))


def _bias_row(rel_bias):
    rb = rel_bias.astype(F32)
    h = rb.shape[0]
    far = rb[:, 2 * MAX_REL:]
    row = jnp.concatenate([
        jnp.broadcast_to(far, (h, LEFT_KEYS - MAX_REL + 1)),
        rb[:, 1:2 * MAX_REL][:, ::-1],
        jnp.broadcast_to(far, (h, ROLL_WIDTH - LEFT_KEYS - MAX_REL)),
    ], axis=1)
    return row.reshape(h, 1, ROLL_WIDTH)


def _mixers(proj4, bias_row, w_spatial, b_spatial):
    _, b, seq, _ = proj4.shape
    n = w_spatial.shape[0]
    blk = lambda sec: pl.BlockSpec((None, None, seq, HEAD_DIM),
                                   lambda bi, h: (sec * n + h, bi, 0, 0))
    out_blk = pl.BlockSpec((None, None, seq, HEAD_DIM), lambda bi, h: (h, bi, 0, 0))
    out = jax.ShapeDtypeStruct((n, b, seq, HEAD_DIM), BF16)
    return pl.pallas_call(
        functools.partial(_mixer_kernel, seq=seq),
        grid=(b, n),
        in_specs=[
            blk(0), blk(1), blk(2),
            pl.BlockSpec((None, 1, ROLL_WIDTH), lambda bi, h: (h, 0, 0)),
            blk(3), blk(4),
            pl.BlockSpec((None, SPATIAL, SPATIAL), lambda bi, h: (h, 0, 0)),
            pl.BlockSpec((None, SPATIAL, 1), lambda bi, h: (h, 0, 0)),
        ],
        out_specs=[out_blk, out_blk],
        out_shape=[out, out],
        scratch_shapes=[pltpu.VMEM((Q_BLOCK, KEY_WINDOW), F32),
                        pltpu.VMEM((seq, 2 * HEAD_DIM), BF16)],
        compiler_params=pltpu.CompilerParams(
            dimension_semantics=("arbitrary", "arbitrary"), vmem_limit_bytes=VMEM_LIMIT_BYTES),
        name="mixers",
    )(proj4, proj4, proj4, bias_row, proj4, proj4, w_spatial,
      b_spatial.reshape(n, SPATIAL, 1))


def _out_proj_kernel(a_ref, m_ref, x_ref, mod_ref, ga_ref, gg_ref, gf_ref, w_ref, x1_ref, h2_ref,
                     *, row_chunk):
    gate_m = mod_ref[2:3, :]
    shift_f = mod_ref[3:4, :]
    scale_f = mod_ref[4:5, :]

    def branch(ref, rows):
        return jnp.concatenate([ref[g, rows, :] for g in range(ref.shape[0])], axis=-1).astype(F32)

    def mixed(r):
        rows = slice(r * row_chunk, (r + 1) * row_chunk)
        a = branch(a_ref, rows)
        m = branch(m_ref, rows)
        return jnp.concatenate(
            [(a * _rms_scale(a) * ga_ref[...]).astype(BF16),
             (m * _rms_scale(m) * gg_ref[...]).astype(BF16)], axis=-1)

    def project(mix):
        return jnp.dot(mix, w_ref[...], preferred_element_type=F32)

    def finish(r, y):
        rows = slice(r * row_chunk, (r + 1) * row_chunk)
        x1 = x_ref[rows, :] + gate_m * y
        x1_ref[rows, :] = x1
        h2 = x1 * _rms_scale(x1) * gf_ref[...]
        h2_ref[rows, :] = (h2 * (1.0 + scale_f) + shift_f).astype(BF16)

    n_chunks = x_ref.shape[0] // row_chunk
    mixes = [mixed(0)]
    ys = []
    for r in range(n_chunks + 1):
        if r + 1 < n_chunks:
            mixes.append(mixed(r + 1))
        if r < n_chunks:
            ys.append(project(mixes.pop(0)))
        if r >= 1:
            finish(r - 1, ys.pop(0))


def _out_proj(attn3, gm3, x2, mod3, ga, gg, gf, w_out, seq, tm=512, row_chunk=256):
    t, d = x2.shape
    wa = attn3.shape[0] * HEAD_DIM
    wg = gm3.shape[0] * HEAD_DIM
    steps_per_batch = seq // tm
    return pl.pallas_call(
        functools.partial(_out_proj_kernel, row_chunk=row_chunk),
        grid=(t // tm,),
        in_specs=[
            pl.BlockSpec((attn3.shape[0], tm, HEAD_DIM), lambda i: (0, i, 0)),
            pl.BlockSpec((gm3.shape[0], tm, HEAD_DIM), lambda i: (0, i, 0)),
            pl.BlockSpec((tm, d), lambda i: (i, 0)),
            pl.BlockSpec((None, N_MOD, d), lambda i: (i // steps_per_batch, 0, 0)),
            pl.BlockSpec((1, wa), lambda i: (0, 0)),
            pl.BlockSpec((1, wg), lambda i: (0, 0)),
            pl.BlockSpec((1, d), lambda i: (0, 0)),
            pl.BlockSpec((wa + wg, d), lambda i: (0, 0)),
        ],
        out_specs=[
            pl.BlockSpec((tm, d), lambda i: (i, 0)),
            pl.BlockSpec((tm, d), lambda i: (i, 0)),
        ],
        out_shape=[jax.ShapeDtypeStruct((t, d), F32), jax.ShapeDtypeStruct((t, d), BF16)],
        compiler_params=pltpu.CompilerParams(
            dimension_semantics=("arbitrary",), vmem_limit_bytes=VMEM_LIMIT_BYTES),
        name="out_proj",
    )(attn3, gm3, x2, mod3, ga, gg, gf, w_out)


def _ffn_kernel(h_ref, w1_ref, w2_ref, x1_hbm, mod_ref, o_ref, x1_buf, x1_sem):
    i = pl.program_id(0)
    j = pl.program_id(1)
    last = pl.num_programs(1) - 1
    tm = o_ref.shape[0]

    x1_copy = pltpu.make_async_copy(x1_hbm.at[pl.ds(i * tm, tm), :], x1_buf, x1_sem)

    def partial_out():
        a = jnp.dot(h_ref[...], w1_ref[...], preferred_element_type=F32)
        a = jnp.square(jnp.maximum(a, 0.0)).astype(BF16)
        return jnp.dot(a, w2_ref[...], preferred_element_type=F32)

    @pl.when(j == 0)
    def _():
        x1_copy.start()
        o_ref[...] = partial_out()

    @pl.when((j > 0) & (j < last))
    def _():
        o_ref[...] += partial_out()

    @pl.when(j == last)
    def _():
        gate_f = mod_ref[5:6, :]
        x1_copy.wait()
        o_ref[...] = x1_buf[...] + gate_f * (o_ref[...] + partial_out())


def _ffn(h2, w1, w2, x1, mod3, seq, tm=1024, tf=512):
    t, d = h2.shape
    f = w1.shape[1]
    steps_per_batch = seq // tm
    assert f // tf >= 2, "first and last d_ff tiles must be distinct steps"
    return pl.pallas_call(
        _ffn_kernel,
        grid=(t // tm, f // tf),
        in_specs=[
            pl.BlockSpec((tm, d), lambda i, j: (i, 0)),
            pl.BlockSpec((d, tf), lambda i, j: (0, j)),
            pl.BlockSpec((tf, d), lambda i, j: (j, 0)),
            pl.BlockSpec(memory_space=pl.ANY),
            pl.BlockSpec((None, N_MOD, d), lambda i, j: (i // steps_per_batch, 0, 0)),
        ],
        out_specs=pl.BlockSpec((tm, d), lambda i, j: (i, 0)),
        out_shape=jax.ShapeDtypeStruct((t, d), F32),
        scratch_shapes=[pltpu.VMEM((tm, d), F32), pltpu.SemaphoreType.DMA(())],
        compiler_params=pltpu.CompilerParams(
            dimension_semantics=("arbitrary", "arbitrary"), vmem_limit_bytes=VMEM_LIMIT_BYTES),
        name="ffn",
    )(h2, w1, w2, x1, mod3)


def kernel(x, c, w_ada, b_ada, mix_norm_g, w_in, q_norm_g, k_norm_g, rel_bias, gmlp_norm_g,
           w_spatial, b_spatial, attn_out_g, gmlp_out_g, w_out, ff_norm_g, w_ff1, w_ff2):
    b, seq, d = x.shape
    depth = w_ada.shape[0]
    n_heads = rel_bias.shape[1]
    n_groups = w_spatial.shape[1]
    attn_w = n_heads * HEAD_DIM
    gmlp_w = n_groups * HEAD_DIM
    assert n_heads == n_groups and w_in.shape[2] == 3 * attn_w + 2 * gmlp_w

    x2 = x.reshape(b * seq, d)
    for l in range(depth):
        mod3 = _ada_mod(c, w_ada[l], b_ada[l]).reshape(b, N_MOD, d)
        sec_gain = jnp.stack([
            jnp.tile(q_norm_g[l], n_heads), jnp.tile(k_norm_g[l], n_heads),
            gmlp_norm_g[l].reshape(-1)]).astype(F32)
        proj, (w_out_b, w_ff1_b, w_ff2_b) = _in_proj(
            x2, mod3, mix_norm_g[l].reshape(1, d), w_in[l].astype(BF16), sec_gain, seq,
            side_weights=(w_out[l], w_ff1[l], w_ff2[l]))
        proj4 = proj.reshape(-1, b, seq, HEAD_DIM)
        attn, gm = _mixers(proj4, _bias_row(rel_bias[l]), w_spatial[l], b_spatial[l])
        x1, h2 = _out_proj(
            attn.reshape(n_heads, b * seq, HEAD_DIM), gm.reshape(n_groups, b * seq, HEAD_DIM),
            x2, mod3,
            attn_out_g[l].reshape(1, attn_w), gmlp_out_g[l].reshape(1, gmlp_w),
            ff_norm_g[l].reshape(1, d), w_out_b, seq)
        x2 = _ffn(h2, w_ff1_b, w_ff2_b, x1, mod3, seq)
    return x2.reshape(b, seq, d)
```

```python
import functools

import jax
import jax.numpy as jnp
from jax import lax
from jax.experimental import pallas as pl
from jax.experimental.pallas import tpu as pltpu

CHUNK = 64
LEFT_CHUNKS = 8
HEAD_DIM = 128
MAX_REL = 128
SPATIAL = 128
N_MOD = 6
EPS = 1e-6
NEG_INF = -1e30
LOG2_E = 1.4426950408889634

Q_BLOCK = 2 * CHUNK
KEY_WINDOW = (LEFT_CHUNKS + 2) * CHUNK
LEFT_KEYS = LEFT_CHUNKS * CHUNK
ROLL_WIDTH = 1024
SCORE_LOOKAHEAD = 2

VMEM_LIMIT_BYTES = 56 * 1024 * 1024

BF16 = jnp.bfloat16
F32 = jnp.float32


def _compiler_params(n_grid_axes):
    return pltpu.CompilerParams(
        dimension_semantics=("arbitrary",) * n_grid_axes, vmem_limit_bytes=VMEM_LIMIT_BYTES)


def _rms_scale(t):
    return lax.rsqrt(jnp.mean(t * t, axis=-1, keepdims=True) + EPS)


def _gelu(t):
    return 0.5 * t * (1.0 + lax.erf(t * (0.5 ** 0.5)))


def _group_norm(t, gain, group):
    parts = []
    for s in range(0, t.shape[-1], group):
        p = t[:, s:s + group]
        parts.append(p * _rms_scale(p) * gain[:, s:s + group])
    return jnp.concatenate(parts, axis=-1)


def _ada_kernel(c_ref, w_ref, b_ref, o_ref):
    c = c_ref[...]
    cond = (c * jax.nn.sigmoid(c)).astype(BF16)
    o_ref[...] = jnp.dot(cond, w_ref[...].astype(BF16), preferred_element_type=F32) + b_ref[...]


def _ada_mod(c, w_ada, b_ada, tn=1024):
    b, d = c.shape
    n = w_ada.shape[1]
    return pl.pallas_call(
        _ada_kernel,
        grid=(n // tn,),
        in_specs=[
            pl.BlockSpec((b, d), lambda j: (0, 0)),
            pl.BlockSpec((d, tn), lambda j: (0, j)),
            pl.BlockSpec((1, tn), lambda j: (0, j)),
        ],
        out_specs=pl.BlockSpec((b, tn), lambda j: (0, j)),
        out_shape=jax.ShapeDtypeStruct((b, n), F32),
        compiler_params=_compiler_params(1),
        name="ada_mod",
    )(c, w_ada, b_ada.reshape(1, n))


def _in_proj_kernel(x_ref, mod_ref, g_ref, w_ref, sg_ref, *rest, sec, n_side):
    side_in, o_ref, side_out = rest[:n_side], rest[n_side], rest[n_side + 1:]
    for src, dst in zip(side_in, side_out):
        dst[...] = src[...].astype(BF16)

    x = x_ref[...]
    shift = mod_ref[0:1, :]
    scale = mod_ref[1:2, :]
    h = x * _rms_scale(x) * g_ref[...]
    h = (h * (1.0 + scale) + shift).astype(BF16)

    def section(s):
        return jnp.dot(h, w_ref[:, s * sec:(s + 1) * sec], preferred_element_type=F32)

    def put(s, val):
        o_ref[:, s * sec:(s + 1) * sec] = val.astype(BF16)

    put(4, _group_norm(_gelu(section(4)), sg_ref[2:3, :], HEAD_DIM))
    put(3, _gelu(section(3)))
    put(0, _group_norm(section(0), sg_ref[0:1, :], HEAD_DIM))
    put(1, _group_norm(section(1), sg_ref[1:2, :], HEAD_DIM))
    put(2, section(2))


def _in_proj(x2, mod3, g, w_in, sec_gain, seq, side_weights, tm=256):
    t, d = x2.shape
    n = w_in.shape[1]
    sec = sec_gain.shape[-1]
    steps = t // tm
    steps_per_batch = seq // tm
    side_specs = [pl.BlockSpec((w.shape[0] // steps, w.shape[1]), lambda i: (i, 0))
                  for w in side_weights]
    outs = pl.pallas_call(
        functools.partial(_in_proj_kernel, sec=sec, n_side=len(side_weights)),
        grid=(steps,),
        in_specs=[
            pl.BlockSpec((tm, d), lambda i: (i, 0)),
            pl.BlockSpec((None, N_MOD, d), lambda i: (i // steps_per_batch, 0, 0)),
            pl.BlockSpec((1, d), lambda i: (0, 0)),
            pl.BlockSpec((d, n), lambda i: (0, 0), pipeline_mode=pl.Buffered(1)),
            pl.BlockSpec(sec_gain.shape, lambda i: (0, 0)),
        ] + side_specs,
        out_specs=[pl.BlockSpec((tm, n), lambda i: (i, 0))] + side_specs,
        out_shape=[jax.ShapeDtypeStruct((t, n), BF16)]
        + [jax.ShapeDtypeStruct(w.shape, BF16) for w in side_weights],
        compiler_params=_compiler_params(1),
        name="in_proj",
    )(x2, mod3, g, w_in, sec_gain, *side_weights)
    return outs[0], outs[1:]


def _mixer_kernel(q_ref, k_ref, v_ref, base_ref, u_ref, z_ref, ws_ref, bs_ref,
                  o_ref, gm_ref, bias_ref, vext_ref, *, seq):
    _attention(q_ref, k_ref, v_ref, base_ref, o_ref, bias_ref, vext_ref, seq)
    _spatial_gate(u_ref, z_ref, ws_ref, bs_ref, gm_ref, seq)


def _spatial_gate(u_ref, z_ref, w_ref, b_ref, o_ref, seq):
    t = lax.broadcasted_iota(jnp.int32, (SPATIAL, SPATIAL), 0)
    s = lax.broadcasted_iota(jnp.int32, (SPATIAL, SPATIAL), 1)
    w = jnp.where((t // CHUNK) >= (s // CHUNK), w_ref[...], 0.0).astype(BF16)
    bias = b_ref[...]
    for n in range(seq // SPATIAL):
        rows = slice(n * SPATIAL, (n + 1) * SPATIAL)
        y = jnp.dot(w, z_ref[rows, :], preferred_element_type=F32) + bias
        o_ref[rows, :] = (u_ref[rows, :].astype(F32) * y).astype(BF16)


def _attention(q_ref, k_ref, v_ref, base_ref, o_ref, bias_ref, vext_ref, seq):
    scale = HEAD_DIM ** -0.5 * LOG2_E
    tab = pltpu.roll(jnp.broadcast_to(base_ref[...], (Q_BLOCK, ROLL_WIDTH)), 0, 1,
                     stride=1, stride_axis=0)[:, :KEY_WINDOW]
    r = lax.broadcasted_iota(jnp.int32, (Q_BLOCK, KEY_WINDOW), 0)
    col = lax.broadcasted_iota(jnp.int32, (Q_BLOCK, KEY_WINDOW), 1)
    qc = r // CHUNK
    kc = col // CHUNK - LEFT_CHUNKS
    bias_ref[...] = jnp.where((kc <= qc) & (kc >= qc - LEFT_CHUNKS), tab * LOG2_E, NEG_INF)
    vext_ref[:, :HEAD_DIM] = v_ref[...]
    vext_ref[:, HEAD_DIM:] = jnp.ones((seq, HEAD_DIM), BF16)

    def window(qb):
        p0 = qb * Q_BLOCK
        ws = max(0, p0 - LEFT_KEYS)
        return p0, ws, ws - (p0 - LEFT_KEYS)

    def scores(qb):
        p0, ws, co = window(qb)
        s = lax.dot_general(q_ref[p0:p0 + Q_BLOCK, :], k_ref[ws:p0 + Q_BLOCK, :],
                            (((1,), (1,)), ((), ())), preferred_element_type=F32)
        return s * scale + bias_ref[:, co:KEY_WINDOW]

    def finish(qb, s):
        p0, ws, _ = window(qb)
        m = jnp.max(s, axis=-1, keepdims=True)
        p = jnp.exp2(s - m).astype(BF16)
        o = jnp.dot(p, vext_ref[ws:p0 + Q_BLOCK, :], preferred_element_type=F32)
        o_ref[p0:p0 + Q_BLOCK, :] = (o[:, :HEAD_DIM] / o[:, HEAD_DIM:]).astype(BF16)

    n_blocks = seq // Q_BLOCK
    pending = [scores(qb) for qb in range(SCORE_LOOKAHEAD)]
    for qb in range(n_blocks):
        if qb + SCORE_LOOKAHEAD < n_blocks:
            pending.append(scores(qb + SCORE_LOOKAHEAD))
        finish(qb, pending.pop(0))


def _bias_row(rel_bias):
    rb = rel_bias.astype(F32)
    h = rb.shape[0]
    far = rb[:, 2 * MAX_REL:]
    row = jnp.concatenate([
        jnp.broadcast_to(far, (h, LEFT_KEYS - MAX_REL + 1)),
        rb[:, 1:2 * MAX_REL][:, ::-1],
        jnp.broadcast_to(far, (h, ROLL_WIDTH - LEFT_KEYS - MAX_REL)),
    ], axis=1)
    return row.reshape(h, 1, ROLL_WIDTH)


def _mixers(proj3, bias_row, w_spatial, b_spatial):
    b, seq, _ = proj3.shape
    n = w_spatial.shape[0]
    blk = lambda sec: pl.BlockSpec((None, seq, HEAD_DIM), lambda bi, h: (bi, 0, sec * n + h))
    out_blk = pl.BlockSpec((None, seq, HEAD_DIM), lambda bi, h: (bi, 0, h))
    out = jax.ShapeDtypeStruct((b, seq, n * HEAD_DIM), BF16)
    return pl.pallas_call(
        functools.partial(_mixer_kernel, seq=seq),
        grid=(b, n),
        in_specs=[
            blk(0), blk(1), blk(2),
            pl.BlockSpec((None, 1, ROLL_WIDTH), lambda bi, h: (h, 0, 0)),
            blk(3), blk(4),
            pl.BlockSpec((None, SPATIAL, SPATIAL), lambda bi, h: (h, 0, 0)),
            pl.BlockSpec((None, SPATIAL, 1), lambda bi, h: (h, 0, 0)),
        ],
        out_specs=[out_blk, out_blk],
        out_shape=[out, out],
        scratch_shapes=[pltpu.VMEM((Q_BLOCK, KEY_WINDOW), F32),
                        pltpu.VMEM((seq, 2 * HEAD_DIM), BF16)],
        compiler_params=_compiler_params(2),
        name="mixers",
    )(proj3, proj3, proj3, bias_row, proj3, proj3, w_spatial,
      b_spatial.reshape(n, SPATIAL, 1))


def _out_proj_kernel(a_ref, m_ref, x_ref, mod_ref, ga_ref, gg_ref, gf_ref, w_ref, x1_ref, h2_ref):
    a = a_ref[...].astype(F32)
    m = m_ref[...].astype(F32)
    mix = jnp.concatenate(
        [(a * _rms_scale(a) * ga_ref[...]).astype(BF16),
         (m * _rms_scale(m) * gg_ref[...]).astype(BF16)], axis=-1)
    y = jnp.dot(mix, w_ref[...], preferred_element_type=F32)
    gate_m = mod_ref[2:3, :]
    shift_f = mod_ref[3:4, :]
    scale_f = mod_ref[4:5, :]
    x1 = x_ref[...] + gate_m * y
    x1_ref[...] = x1
    h2 = x1 * _rms_scale(x1) * gf_ref[...]
    h2_ref[...] = (h2 * (1.0 + scale_f) + shift_f).astype(BF16)


def _out_proj(attn2, gm2, x2, mod3, ga, gg, gf, w_out, seq, tm=512):
    t, d = x2.shape
    wa = attn2.shape[1]
    wg = gm2.shape[1]
    steps_per_batch = seq // tm
    return pl.pallas_call(
        _out_proj_kernel,
        grid=(t // tm,),
        in_specs=[
            pl.BlockSpec((tm, wa), lambda i: (i, 0)),
            pl.BlockSpec((tm, wg), lambda i: (i, 0)),
            pl.BlockSpec((tm, d), lambda i: (i, 0)),
            pl.BlockSpec((None, N_MOD, d), lambda i: (i // steps_per_batch, 0, 0)),
            pl.BlockSpec((1, wa), lambda i: (0, 0)),
            pl.BlockSpec((1, wg), lambda i: (0, 0)),
            pl.BlockSpec((1, d), lambda i: (0, 0)),
            pl.BlockSpec((wa + wg, d), lambda i: (0, 0)),
        ],
        out_specs=[
            pl.BlockSpec((tm, d), lambda i: (i, 0)),
            pl.BlockSpec((tm, d), lambda i: (i, 0)),
        ],
        out_shape=[jax.ShapeDtypeStruct((t, d), F32), jax.ShapeDtypeStruct((t, d), BF16)],
        compiler_params=_compiler_params(1),
        name="out_proj",
    )(attn2, gm2, x2, mod3, ga, gg, gf, w_out)


def _ffn_kernel(h_ref, w1_ref, w2_ref, x1_hbm, mod_ref, o_ref, x1_buf, x1_sem):
    i = pl.program_id(0)
    j = pl.program_id(1)
    last = pl.num_programs(1) - 1
    tm = o_ref.shape[0]

    x1_copy = pltpu.make_async_copy(x1_hbm.at[pl.ds(i * tm, tm), :], x1_buf, x1_sem)

    def partial_out():
        a = jnp.dot(h_ref[...], w1_ref[...], preferred_element_type=F32)
        a = jnp.square(jnp.maximum(a, 0.0)).astype(BF16)
        return jnp.dot(a, w2_ref[...], preferred_element_type=F32)

    @pl.when(j == 0)
    def _():
        x1_copy.start()
        o_ref[...] = partial_out()

    @pl.when((j > 0) & (j < last))
    def _():
        o_ref[...] += partial_out()

    @pl.when(j == last)
    def _():
        gate_f = mod_ref[5:6, :]
        x1_copy.wait()
        o_ref[...] = x1_buf[...] + gate_f * (o_ref[...] + partial_out())


def _ffn(h2, w1, w2, x1, mod3, seq, tm=1024, tf=1024):
    t, d = h2.shape
    f = w1.shape[1]
    steps_per_batch = seq // tm
    assert f // tf >= 2, "first and last d_ff tiles must be distinct steps"
    return pl.pallas_call(
        _ffn_kernel,
        grid=(t // tm, f // tf),
        in_specs=[
            pl.BlockSpec((tm, d), lambda i, j: (i, 0)),
            pl.BlockSpec((d, tf), lambda i, j: (0, j)),
            pl.BlockSpec((tf, d), lambda i, j: (j, 0)),
            pl.BlockSpec(memory_space=pl.ANY),
            pl.BlockSpec((None, N_MOD, d), lambda i, j: (i // steps_per_batch, 0, 0)),
        ],
        out_specs=pl.BlockSpec((tm, d), lambda i, j: (i, 0)),
        out_shape=jax.ShapeDtypeStruct((t, d), F32),
        scratch_shapes=[pltpu.VMEM((tm, d), F32), pltpu.SemaphoreType.DMA(())],
        compiler_params=_compiler_params(2),
        name="ffn",
    )(h2, w1, w2, x1, mod3)


def kernel(x, c, w_ada, b_ada, mix_norm_g, w_in, q_norm_g, k_norm_g, rel_bias, gmlp_norm_g,
           w_spatial, b_spatial, attn_out_g, gmlp_out_g, w_out, ff_norm_g, w_ff1, w_ff2):
    b, seq, d = x.shape
    depth = w_ada.shape[0]
    n_heads = rel_bias.shape[1]
    n_groups = w_spatial.shape[1]
    attn_w = n_heads * HEAD_DIM
    gmlp_w = n_groups * HEAD_DIM
    assert n_heads == n_groups and w_in.shape[2] == 3 * attn_w + 2 * gmlp_w

    x2 = x.reshape(b * seq, d)
    for l in range(depth):
        mod3 = _ada_mod(c, w_ada[l], b_ada[l]).reshape(b, N_MOD, d)
        sec_gain = jnp.stack([
            jnp.tile(q_norm_g[l], n_heads), jnp.tile(k_norm_g[l], n_heads),
            gmlp_norm_g[l].reshape(-1)]).astype(F32)
        proj, (w_out_b, w_ff1_b, w_ff2_b) = _in_proj(
            x2, mod3, mix_norm_g[l].reshape(1, d), w_in[l].astype(BF16), sec_gain, seq,
            side_weights=(w_out[l], w_ff1[l], w_ff2[l]))
        proj3 = proj.reshape(b, seq, -1)
        attn, gm = _mixers(proj3, _bias_row(rel_bias[l]), w_spatial[l], b_spatial[l])
        x1, h2 = _out_proj(
            attn.reshape(b * seq, attn_w), gm.reshape(b * seq, gmlp_w), x2, mod3,
            attn_out_g[l].reshape(1, attn_w), gmlp_out_g[l].reshape(1, gmlp_w),
            ff_norm_g[l].reshape(1, d), w_out_b, seq)
        x2 = _ffn(h2, w_ff1_b, w_ff2_b, x1, mod3, seq)
    return x2.reshape(b, seq, d)
```

```python
import functools

import jax
import jax.numpy as jnp
from jax import lax
from jax.experimental import pallas as pl
from jax.experimental.pallas import tpu as pltpu

CHUNK = 64
LEFT_CHUNKS = 8
HEAD_DIM = 128
MAX_REL = 128
SPATIAL = 128
N_MOD = 6
EPS = 1e-6
NEG_INF = -1e30
LOG2_E = 1.4426950408889634

Q_BLOCK = 2 * CHUNK
KEY_WINDOW = LEFT_CHUNKS * CHUNK + Q_BLOCK
LEFT_KEYS = LEFT_CHUNKS * CHUNK
ROLL_WIDTH = 1024
SCORE_LOOKAHEAD = 2

VMEM_LIMIT_BYTES = 56 * 1024 * 1024

BF16 = jnp.bfloat16
F32 = jnp.float32


def _compiler_params(n_grid_axes):
    return pltpu.CompilerParams(
        dimension_semantics=("arbitrary",) * n_grid_axes, vmem_limit_bytes=VMEM_LIMIT_BYTES)


def _rms_scale(t):
    return lax.rsqrt(jnp.mean(t * t, axis=-1, keepdims=True) + EPS)


def _gelu(t):
    return 0.5 * t * (1.0 + lax.erf(t * (0.5 ** 0.5)))


def _group_norm(t, gain, group):
    parts = []
    for s in range(0, t.shape[-1], group):
        p = t[:, s:s + group]
        parts.append(p * _rms_scale(p) * gain[:, s:s + group])
    return jnp.concatenate(parts, axis=-1)


def _ada_kernel(c_ref, w_ref, b_ref, o_ref):
    c = c_ref[...]
    cond = (c * jax.nn.sigmoid(c)).astype(BF16)
    o_ref[...] = jnp.dot(cond, w_ref[...].astype(BF16), preferred_element_type=F32) + b_ref[...]


def _ada_mod(c, w_ada, b_ada, tn=1024):
    b, d = c.shape
    n = w_ada.shape[1]
    return pl.pallas_call(
        _ada_kernel,
        grid=(n // tn,),
        in_specs=[
            pl.BlockSpec((b, d), lambda j: (0, 0)),
            pl.BlockSpec((d, tn), lambda j: (0, j)),
            pl.BlockSpec((1, tn), lambda j: (0, j)),
        ],
        out_specs=pl.BlockSpec((b, tn), lambda j: (0, j)),
        out_shape=jax.ShapeDtypeStruct((b, n), F32),
        compiler_params=_compiler_params(1),
        name="ada_mod",
    )(c, w_ada, b_ada.reshape(1, n))


def _in_proj_kernel(x_ref, mod_ref, g_ref, w_hbm, sg_ref, *rest, sec, n_side, w_chunk):
    side_in, o_ref, side_out = rest[:n_side], rest[n_side], rest[n_side + 1:2 * n_side + 1]
    w_ref, stage_ref, stage_sem = rest[2 * n_side + 1:]

    @pl.when(pl.program_id(0) == 0)
    def _():
        n_chunks = w_ref.shape[0] // w_chunk

        def chunk_copy(c):
            return pltpu.make_async_copy(
                w_hbm.at[pl.ds(c * w_chunk, w_chunk), :], stage_ref.at[c % 2], stage_sem.at[c % 2])

        chunk_copy(0).start()
        for c in range(n_chunks):
            if c + 1 < n_chunks:
                chunk_copy(c + 1).start()
            chunk_copy(c).wait()
            w_ref[c * w_chunk:(c + 1) * w_chunk, :] = stage_ref[c % 2].astype(BF16)

    for src, dst in zip(side_in, side_out):
        dst[...] = src[...].astype(BF16)

    x = x_ref[...]
    shift = mod_ref[0:1, :]
    scale = mod_ref[1:2, :]
    h = x * _rms_scale(x) * g_ref[...]
    h = (h * (1.0 + scale) + shift).astype(BF16)

    def section(s):
        return jnp.dot(h, w_ref[:, s * sec:(s + 1) * sec], preferred_element_type=F32)

    def put(s, val):
        o_ref[:, s * sec:(s + 1) * sec] = val.astype(BF16)

    put(4, _group_norm(_gelu(section(4)), sg_ref[2:3, :], HEAD_DIM))
    put(3, _gelu(section(3)))
    put(0, _group_norm(section(0), sg_ref[0:1, :], HEAD_DIM))
    put(1, _group_norm(section(1), sg_ref[1:2, :], HEAD_DIM))
    put(2, section(2))


def _in_proj(x2, mod3, g, w_in, sec_gain, seq, side_weights, tm=256, w_chunk=128):
    t, d = x2.shape
    n = w_in.shape[1]
    sec = sec_gain.shape[-1]
    steps = t // tm
    steps_per_batch = seq // tm
    side_specs = [pl.BlockSpec((w.shape[0] // steps, w.shape[1]), lambda i: (i, 0))
                  for w in side_weights]
    outs = pl.pallas_call(
        functools.partial(_in_proj_kernel, sec=sec, n_side=len(side_weights), w_chunk=w_chunk),
        grid=(steps,),
        in_specs=[
            pl.BlockSpec((tm, d), lambda i: (i, 0)),
            pl.BlockSpec((None, N_MOD, d), lambda i: (i // steps_per_batch, 0, 0)),
            pl.BlockSpec((1, d), lambda i: (0, 0)),
            pl.BlockSpec(memory_space=pl.ANY),
            pl.BlockSpec(sec_gain.shape, lambda i: (0, 0)),
        ] + side_specs,
        out_specs=[pl.BlockSpec((tm, n), lambda i: (i, 0))] + side_specs,
        out_shape=[jax.ShapeDtypeStruct((t, n), BF16)]
        + [jax.ShapeDtypeStruct(w.shape, BF16) for w in side_weights],
        scratch_shapes=[pltpu.VMEM((d, n), BF16), pltpu.VMEM((2, w_chunk, n), F32),
                        pltpu.SemaphoreType.DMA((2,))],
        compiler_params=_compiler_params(1),
        name="in_proj",
    )(x2, mod3, g, w_in, sec_gain, *side_weights)
    return outs[0], outs[1:]


def _mixer_kernel(q_ref, k_ref, v_ref, base_ref, u_ref, z_ref, ws_ref, bs_ref,
                  o_ref, gm_ref, bias_ref, vext_ref, *, seq):
    _attention(q_ref, k_ref, v_ref, base_ref, o_ref, bias_ref, vext_ref, seq)
    _spatial_gate(u_ref, z_ref, ws_ref, bs_ref, gm_ref, seq)


def _spatial_gate(u_ref, z_ref, w_ref, b_ref, o_ref, seq):
    t = lax.broadcasted_iota(jnp.int32, (SPATIAL, SPATIAL), 0)
    s = lax.broadcasted_iota(jnp.int32, (SPATIAL, SPATIAL), 1)
    w = jnp.where((t // CHUNK) >= (s // CHUNK), w_ref[...], 0.0).astype(BF16)
    bias = b_ref[...]
    for n in range(seq // SPATIAL):
        rows = slice(n * SPATIAL, (n + 1) * SPATIAL)
        y = jnp.dot(w, z_ref[rows, :], preferred_element_type=F32) + bias
        o_ref[rows, :] = (u_ref[rows, :].astype(F32) * y).astype(BF16)


def _attention(q_ref, k_ref, v_ref, base_ref, o_ref, bias_ref, vext_ref, seq):
    scale = HEAD_DIM ** -0.5 * LOG2_E
    @pl.when(pl.program_id(1) == 0)
    def _():
        tab = pltpu.roll(jnp.broadcast_to(base_ref[...], (Q_BLOCK, ROLL_WIDTH)), 0, 1,
                         stride=1, stride_axis=0)[:, :KEY_WINDOW]
        r = lax.broadcasted_iota(jnp.int32, (Q_BLOCK, KEY_WINDOW), 0)
        col = lax.broadcasted_iota(jnp.int32, (Q_BLOCK, KEY_WINDOW), 1)
        qc = r // CHUNK
        kc = col // CHUNK - LEFT_CHUNKS
        bias_ref[...] = jnp.where((kc <= qc) & (kc >= qc - LEFT_CHUNKS), tab * LOG2_E, NEG_INF)

    vext_ref[:, :HEAD_DIM] = v_ref[...]
    vext_ref[:, HEAD_DIM:] = jnp.ones((seq, HEAD_DIM), BF16)

    def window(qb):
        p0 = qb * Q_BLOCK
        ws = max(0, p0 - LEFT_KEYS)
        return p0, ws, ws - (p0 - LEFT_KEYS)

    def scores(qb):
        p0, ws, co = window(qb)
        s = lax.dot_general(q_ref[p0:p0 + Q_BLOCK, :], k_ref[ws:p0 + Q_BLOCK, :],
                            (((1,), (1,)), ((), ())), preferred_element_type=F32)
        return s * scale + bias_ref[:, co:KEY_WINDOW]

    def finish(qb, s):
        p0, ws, _ = window(qb)
        m = jnp.max(s, axis=-1, keepdims=True)
        p = jnp.exp2(s - m).astype(BF16)
        o = jnp.dot(p, vext_ref[ws:p0 + Q_BLOCK, :], preferred_element_type=F32)
        o_ref[p0:p0 + Q_BLOCK, :] = (o[:, :HEAD_DIM] / o[:, HEAD_DIM:]).astype(BF16)

    n_blocks = seq // Q_BLOCK
    pending = [scores(qb) for qb in range(SCORE_LOOKAHEAD)]
    for qb in range(n_blocks):
        if qb + SCORE_LOOKAHEAD < n_blocks:
            pending.append(scores(qb + SCORE_LOOKAHEAD))
        finish(qb, pending.pop(0))


def _bias_row(rel_bias):
    rb = rel_bias.astype(F32)
    h = rb.shape[0]
    far = rb[:, 2 * MAX_REL:]
    row = jnp.concatenate([
        jnp.broadcast_to(far, (h, LEFT_KEYS - MAX_REL + 1)),
        rb[:, 1:2 * MAX_REL][:, ::-1],
        jnp.broadcast_to(rb[:, :1], (h, KEY_WINDOW - LEFT_KEYS - MAX_REL)),
        jnp.broadcast_to(far, (h, ROLL_WIDTH - KEY_WINDOW)),
    ], axis=1)
    return row.reshape(h, 1, ROLL_WIDTH)


def _mixers(proj3, bias_row, w_spatial, b_spatial):
    b, seq, _ = proj3.shape
    n = w_spatial.shape[0]
    blk = lambda sec: pl.BlockSpec((None, seq, HEAD_DIM), lambda h, bi: (bi, 0, sec * n + h))
    out_blk = pl.BlockSpec((None, seq, HEAD_DIM), lambda h, bi: (bi, 0, h))
    out = jax.ShapeDtypeStruct((b, seq, n * HEAD_DIM), BF16)
    return pl.pallas_call(
        functools.partial(_mixer_kernel, seq=seq),
        grid=(n, b),
        in_specs=[
            blk(0), blk(1), blk(2),
            pl.BlockSpec((None, 1, ROLL_WIDTH), lambda h, bi: (h, 0, 0)),
            blk(3), blk(4),
            pl.BlockSpec((None, SPATIAL, SPATIAL), lambda h, bi: (h, 0, 0)),
            pl.BlockSpec((None, SPATIAL, 1), lambda h, bi: (h, 0, 0)),
        ],
        out_specs=[out_blk, out_blk],
        out_shape=[out, out],
        scratch_shapes=[pltpu.VMEM((Q_BLOCK, KEY_WINDOW), F32),
                        pltpu.VMEM((seq, 2 * HEAD_DIM), BF16)],
        compiler_params=_compiler_params(2),
        name="mixers",
    )(proj3, proj3, proj3, bias_row, proj3, proj3, w_spatial,
      b_spatial.reshape(n, SPATIAL, 1))


def _out_proj_kernel(a_ref, m_ref, x_ref, mod_ref, ga_ref, gg_ref, gf_ref, w_ref, x1_ref, h2_ref):
    a = a_ref[...].astype(F32)
    m = m_ref[...].astype(F32)
    mix = jnp.concatenate(
        [(a * _rms_scale(a) * ga_ref[...]).astype(BF16),
         (m * _rms_scale(m) * gg_ref[...]).astype(BF16)], axis=-1)
    y = jnp.dot(mix, w_ref[...], preferred_element_type=F32)
    gate_m = mod_ref[2:3, :]
    shift_f = mod_ref[3:4, :]
    scale_f = mod_ref[4:5, :]
    x1 = x_ref[...] + gate_m * y
    x1_ref[...] = x1
    h2 = x1 * _rms_scale(x1) * gf_ref[...]
    h2_ref[...] = (h2 * (1.0 + scale_f) + shift_f).astype(BF16)


def _out_proj(attn2, gm2, x2, mod3, ga, gg, gf, w_out, seq, tm=512):
    t, d = x2.shape
    wa = attn2.shape[1]
    wg = gm2.shape[1]
    steps_per_batch = seq // tm
    return pl.pallas_call(
        _out_proj_kernel,
        grid=(t // tm,),
        in_specs=[
            pl.BlockSpec((tm, wa), lambda i: (i, 0)),
            pl.BlockSpec((tm, wg), lambda i: (i, 0)),
            pl.BlockSpec((tm, d), lambda i: (i, 0)),
            pl.BlockSpec((None, N_MOD, d), lambda i: (i // steps_per_batch, 0, 0)),
            pl.BlockSpec((1, wa), lambda i: (0, 0)),
            pl.BlockSpec((1, wg), lambda i: (0, 0)),
            pl.BlockSpec((1, d), lambda i: (0, 0)),
            pl.BlockSpec((wa + wg, d), lambda i: (0, 0)),
        ],
        out_specs=[
            pl.BlockSpec((tm, d), lambda i: (i, 0)),
            pl.BlockSpec((tm, d), lambda i: (i, 0)),
        ],
        out_shape=[jax.ShapeDtypeStruct((t, d), F32), jax.ShapeDtypeStruct((t, d), BF16)],
        compiler_params=_compiler_params(1),
        name="out_proj",
    )(attn2, gm2, x2, mod3, ga, gg, gf, w_out)


def _ffn_kernel(h_ref, w1_ref, w2_ref, x1_hbm, mod_ref, o_ref, x1_buf, x1_sem):
    i = pl.program_id(0)
    j = pl.program_id(1)
    last = pl.num_programs(1) - 1
    tm = o_ref.shape[0]

    x1_copy = pltpu.make_async_copy(x1_hbm.at[pl.ds(i * tm, tm), :], x1_buf, x1_sem)

    def partial_out():
        a = jnp.dot(h_ref[...], w1_ref[...], preferred_element_type=F32)
        a = jnp.square(jnp.maximum(a, 0.0)).astype(BF16)
        return jnp.dot(a, w2_ref[...], preferred_element_type=F32)

    @pl.when(j == 0)
    def _():
        x1_copy.start()
        o_ref[...] = partial_out()

    @pl.when((j > 0) & (j < last))
    def _():
        o_ref[...] += partial_out()

    @pl.when(j == last)
    def _():
        gate_f = mod_ref[5:6, :]
        x1_copy.wait()
        o_ref[...] = x1_buf[...] + gate_f * (o_ref[...] + partial_out())


def _ffn(h2, w1, w2, x1, mod3, seq, tm=1024, tf=1024):
    t, d = h2.shape
    f = w1.shape[1]
    steps_per_batch = seq // tm
    assert f // tf >= 2, "first and last d_ff tiles must be distinct steps"
    return pl.pallas_call(
        _ffn_kernel,
        grid=(t // tm, f // tf),
        in_specs=[
            pl.BlockSpec((tm, d), lambda i, j: (i, 0)),
            pl.BlockSpec((d, tf), lambda i, j: (0, j)),
            pl.BlockSpec((tf, d), lambda i, j: (j, 0)),
            pl.BlockSpec(memory_space=pl.ANY),
            pl.BlockSpec((None, N_MOD, d), lambda i, j: (i // steps_per_batch, 0, 0)),
        ],
        out_specs=pl.BlockSpec((tm, d), lambda i, j: (i, 0)),
        out_shape=jax.ShapeDtypeStruct((t, d), F32),
        scratch_shapes=[pltpu.VMEM((tm, d), F32), pltpu.SemaphoreType.DMA(())],
        compiler_params=_compiler_params(2),
        name="ffn",
    )(h2, w1, w2, x1, mod3)


def kernel(x, c, w_ada, b_ada, mix_norm_g, w_in, q_norm_g, k_norm_g, rel_bias, gmlp_norm_g,
           w_spatial, b_spatial, attn_out_g, gmlp_out_g, w_out, ff_norm_g, w_ff1, w_ff2):
    b, seq, d = x.shape
    depth = w_ada.shape[0]
    n_heads = rel_bias.shape[1]
    n_groups = w_spatial.shape[1]
    attn_w = n_heads * HEAD_DIM
    gmlp_w = n_groups * HEAD_DIM
    assert n_heads == n_groups and w_in.shape[2] == 3 * attn_w + 2 * gmlp_w

    x2 = x.reshape(b * seq, d)
    for l in range(depth):
        mod3 = _ada_mod(c, w_ada[l], b_ada[l]).reshape(b, N_MOD, d)
        sec_gain = jnp.stack([
            jnp.tile(q_norm_g[l], n_heads), jnp.tile(k_norm_g[l], n_heads),
            gmlp_norm_g[l].reshape(-1)]).astype(F32)
        proj, (w_out_b, w_ff1_b, w_ff2_b) = _in_proj(
            x2, mod3, mix_norm_g[l].reshape(1, d), w_in[l], sec_gain, seq,
            side_weights=(w_out[l], w_ff1[l], w_ff2[l]))
        proj3 = proj.reshape(b, seq, -1)
        attn, gm = _mixers(proj3, _bias_row(rel_bias[l]), w_spatial[l], b_spatial[l])
        x1, h2 = _out_proj(
            attn.reshape(b * seq, attn_w), gm.reshape(b * seq, gmlp_w), x2, mod3,
            attn_out_g[l].reshape(1, attn_w), gmlp_out_g[l].reshape(1, gmlp_w),
            ff_norm_g[l].reshape(1, d), w_out_b, seq)
        x2 = _ffn(h2, w_ff1_b, w_ff2_b, x1, mod3, seq)
    return x2.reshape(b, seq, d)
```

```python
import functools

import jax
import jax.numpy as jnp
from jax import lax
from jax.experimental import pallas as pl
from jax.experimental.pallas import tpu as pltpu

CHUNK = 64
LEFT_CHUNKS = 8
HEAD_DIM = 128
MAX_REL = 128
SPATIAL = 128
N_MOD = 6
EPS = 1e-6
NEG_INF = -1e30
LOG2_E = 1.4426950408889634

Q_BLOCK = 2 * CHUNK
KEY_WINDOW = LEFT_CHUNKS * CHUNK + Q_BLOCK
LEFT_KEYS = LEFT_CHUNKS * CHUNK
ROLL_WIDTH = 1024
SCORE_LOOKAHEAD = 2

VMEM_LIMIT_BYTES = 56 * 1024 * 1024

BF16 = jnp.bfloat16
F32 = jnp.float32


def _compiler_params(n_grid_axes):
    return pltpu.CompilerParams(
        dimension_semantics=("arbitrary",) * n_grid_axes, vmem_limit_bytes=VMEM_LIMIT_BYTES)


def _rms_scale(t):
    return lax.rsqrt(jnp.mean(t * t, axis=-1, keepdims=True) + EPS)


def _gelu(t):
    return 0.5 * t * (1.0 + lax.erf(t * (0.5 ** 0.5)))


def _group_norm(t, gain, group):
    parts = []
    for s in range(0, t.shape[-1], group):
        p = t[:, s:s + group]
        parts.append(p * _rms_scale(p) * gain[:, s:s + group])
    return jnp.concatenate(parts, axis=-1)


def _ada_kernel(c_ref, w_ref, b_ref, o_ref):
    c = c_ref[...]
    cond = (c * jax.nn.sigmoid(c)).astype(BF16)
    o_ref[...] = jnp.dot(cond, w_ref[...].astype(BF16), preferred_element_type=F32) + b_ref[...]


def _ada_mod(c, w_ada, b_ada, tn=1024):
    b, d = c.shape
    n = w_ada.shape[1]
    return pl.pallas_call(
        _ada_kernel,
        grid=(n // tn,),
        in_specs=[
            pl.BlockSpec((b, d), lambda j: (0, 0)),
            pl.BlockSpec((d, tn), lambda j: (0, j)),
            pl.BlockSpec((1, tn), lambda j: (0, j)),
        ],
        out_specs=pl.BlockSpec((b, tn), lambda j: (0, j)),
        out_shape=jax.ShapeDtypeStruct((b, n), F32),
        compiler_params=_compiler_params(1),
        name="ada_mod",
    )(c, w_ada, b_ada.reshape(1, n))


def _in_proj_kernel(x_ref, mod_ref, g_ref, w_hbm, sg_ref, *rest, sec, n_side, w_chunk):
    side_in, o_ref, side_out = rest[:n_side], rest[n_side], rest[n_side + 1:2 * n_side + 1]
    w_ref, stage_ref, stage_sem = rest[2 * n_side + 1:]

    @pl.when(pl.program_id(0) == 0)
    def _():
        n_chunks = w_ref.shape[0] // w_chunk

        def chunk_copy(c):
            return pltpu.make_async_copy(
                w_hbm.at[pl.ds(c * w_chunk, w_chunk), :], stage_ref.at[c % 2], stage_sem.at[c % 2])

        chunk_copy(0).start()
        for c in range(n_chunks):
            if c + 1 < n_chunks:
                chunk_copy(c + 1).start()
            chunk_copy(c).wait()
            w_ref[c * w_chunk:(c + 1) * w_chunk, :] = stage_ref[c % 2].astype(BF16)

    for src, dst in zip(side_in, side_out):
        dst[...] = src[...].astype(BF16)

    x = x_ref[...]
    shift = mod_ref[0:1, :]
    scale = mod_ref[1:2, :]
    h = x * _rms_scale(x) * g_ref[...]
    h = (h * (1.0 + scale) + shift).astype(BF16)

    def section(s):
        return jnp.dot(h, w_ref[:, s * sec:(s + 1) * sec], preferred_element_type=F32)

    def put(s, val):
        o_ref[:, s * sec:(s + 1) * sec] = val.astype(BF16)

    put(4, _group_norm(_gelu(section(4)), sg_ref[2:3, :], HEAD_DIM))
    put(3, _gelu(section(3)))
    put(0, _group_norm(section(0), sg_ref[0:1, :], HEAD_DIM))
    put(1, _group_norm(section(1), sg_ref[1:2, :], HEAD_DIM))
    put(2, section(2))


def _in_proj(x2, mod3, g, w_in, sec_gain, seq, side_weights, tm=256, w_chunk=128):
    t, d = x2.shape
    n = w_in.shape[1]
    sec = sec_gain.shape[-1]
    steps = t // tm
    steps_per_batch = seq // tm
    side_specs = [pl.BlockSpec((w.shape[0] // steps, w.shape[1]), lambda i: (i, 0))
                  for w in side_weights]
    outs = pl.pallas_call(
        functools.partial(_in_proj_kernel, sec=sec, n_side=len(side_weights), w_chunk=w_chunk),
        grid=(steps,),
        in_specs=[
            pl.BlockSpec((tm, d), lambda i: (i, 0)),
            pl.BlockSpec((None, N_MOD, d), lambda i: (i // steps_per_batch, 0, 0)),
            pl.BlockSpec((1, d), lambda i: (0, 0)),
            pl.BlockSpec(memory_space=pl.ANY),
            pl.BlockSpec(sec_gain.shape, lambda i: (0, 0)),
        ] + side_specs,
        out_specs=[pl.BlockSpec((tm, n), lambda i: (i, 0))] + side_specs,
        out_shape=[jax.ShapeDtypeStruct((t, n), BF16)]
        + [jax.ShapeDtypeStruct(w.shape, BF16) for w in side_weights],
        scratch_shapes=[pltpu.VMEM((d, n), BF16), pltpu.VMEM((2, w_chunk, n), F32),
                        pltpu.SemaphoreType.DMA((2,))],
        compiler_params=_compiler_params(1),
        name="in_proj",
    )(x2, mod3, g, w_in, sec_gain, *side_weights)
    return outs[0], outs[1:]


def _mixer_kernel(q_ref, k_ref, v_ref, base_ref, u_ref, z_ref, ws_ref, bs_ref,
                  o_ref, gm_ref, bias_ref, vext_ref, *, seq):
    @pl.when(pl.program_id(1) == 0)
    def _():
        for g in range(base_ref.shape[0]):
            _bias_table(base_ref.at[g], bias_ref.at[g])

    for g in range(base_ref.shape[0]):
        lanes = lambda ref: ref.at[:, g * HEAD_DIM:(g + 1) * HEAD_DIM]
        _attention(lanes(q_ref), lanes(k_ref), lanes(v_ref), lanes(o_ref),
                   bias_ref.at[g], vext_ref.at[g], seq)
    for g in range(base_ref.shape[0]):
        lanes = lambda ref: ref.at[:, g * HEAD_DIM:(g + 1) * HEAD_DIM]
        _spatial_gate(lanes(u_ref), lanes(z_ref), ws_ref.at[g], bs_ref.at[g], lanes(gm_ref), seq)


def _spatial_gate(u_ref, z_ref, w_ref, b_ref, o_ref, seq):
    t = lax.broadcasted_iota(jnp.int32, (SPATIAL, SPATIAL), 0)
    s = lax.broadcasted_iota(jnp.int32, (SPATIAL, SPATIAL), 1)
    w = jnp.where((t // CHUNK) >= (s // CHUNK), w_ref[...], 0.0).astype(BF16)
    bias = b_ref[...]
    for n in range(seq // SPATIAL):
        rows = slice(n * SPATIAL, (n + 1) * SPATIAL)
        y = jnp.dot(w, z_ref[rows, :], preferred_element_type=F32) + bias
        o_ref[rows, :] = (u_ref[rows, :].astype(F32) * y).astype(BF16)


def _bias_table(base_ref, bias_ref):
    tab = pltpu.roll(jnp.broadcast_to(base_ref[...], (Q_BLOCK, ROLL_WIDTH)), 0, 1,
                     stride=1, stride_axis=0)[:, :KEY_WINDOW]
    r = lax.broadcasted_iota(jnp.int32, (Q_BLOCK, KEY_WINDOW), 0)
    col = lax.broadcasted_iota(jnp.int32, (Q_BLOCK, KEY_WINDOW), 1)
    qc = r // CHUNK
    kc = col // CHUNK - LEFT_CHUNKS
    bias_ref[...] = jnp.where((kc <= qc) & (kc >= qc - LEFT_CHUNKS), tab * LOG2_E, NEG_INF)


def _attention(q_ref, k_ref, v_ref, o_ref, bias_ref, vext_ref, seq):
    scale = HEAD_DIM ** -0.5 * LOG2_E
    vext_ref[:, :HEAD_DIM] = v_ref[...]
    vext_ref[:, HEAD_DIM:] = jnp.ones((seq, HEAD_DIM), BF16)

    def window(qb):
        p0 = qb * Q_BLOCK
        ws = max(0, p0 - LEFT_KEYS)
        return p0, ws, ws - (p0 - LEFT_KEYS)

    def scores(qb):
        p0, ws, co = window(qb)
        s = lax.dot_general(q_ref[p0:p0 + Q_BLOCK, :], k_ref[ws:p0 + Q_BLOCK, :],
                            (((1,), (1,)), ((), ())), preferred_element_type=F32)
        return s * scale + bias_ref[:, co:KEY_WINDOW]

    def finish(qb, s):
        p0, ws, _ = window(qb)
        m = jnp.max(s, axis=-1, keepdims=True)
        p = jnp.exp2(s - m).astype(BF16)
        o = jnp.dot(p, vext_ref[ws:p0 + Q_BLOCK, :], preferred_element_type=F32)
        o_ref[p0:p0 + Q_BLOCK, :] = (o[:, :HEAD_DIM] / o[:, HEAD_DIM:]).astype(BF16)

    n_blocks = seq // Q_BLOCK
    pending = [scores(qb) for qb in range(SCORE_LOOKAHEAD)]
    for qb in range(n_blocks):
        if qb + SCORE_LOOKAHEAD < n_blocks:
            pending.append(scores(qb + SCORE_LOOKAHEAD))
        finish(qb, pending.pop(0))


def _bias_row(rel_bias):
    rb = rel_bias.astype(F32)
    h = rb.shape[0]
    far = rb[:, 2 * MAX_REL:]
    row = jnp.concatenate([
        jnp.broadcast_to(far, (h, LEFT_KEYS - MAX_REL + 1)),
        rb[:, 1:2 * MAX_REL][:, ::-1],
        jnp.broadcast_to(rb[:, :1], (h, KEY_WINDOW - LEFT_KEYS - MAX_REL)),
        jnp.broadcast_to(far, (h, ROLL_WIDTH - KEY_WINDOW)),
    ], axis=1)
    return row.reshape(h, 1, ROLL_WIDTH)


def _mixers(proj3, bias_row, w_spatial, b_spatial, groups_per_step=4):
    b, seq, _ = proj3.shape
    n = w_spatial.shape[0]
    gps = groups_per_step
    width = gps * HEAD_DIM
    n_steps = n // gps
    blk = lambda sec: pl.BlockSpec((None, seq, width), lambda h, bi: (bi, 0, sec * n_steps + h))
    out_blk = pl.BlockSpec((None, seq, width), lambda h, bi: (bi, 0, h))
    out = jax.ShapeDtypeStruct((b, seq, n * HEAD_DIM), BF16)
    return pl.pallas_call(
        functools.partial(_mixer_kernel, seq=seq),
        grid=(n_steps, b),
        in_specs=[
            blk(0), blk(1), blk(2),
            pl.BlockSpec((gps, 1, ROLL_WIDTH), lambda h, bi: (h, 0, 0)),
            blk(3), blk(4),
            pl.BlockSpec((gps, SPATIAL, SPATIAL), lambda h, bi: (h, 0, 0)),
            pl.BlockSpec((gps, SPATIAL, 1), lambda h, bi: (h, 0, 0)),
        ],
        out_specs=[out_blk, out_blk],
        out_shape=[out, out],
        scratch_shapes=[pltpu.VMEM((gps, Q_BLOCK, KEY_WINDOW), F32),
                        pltpu.VMEM((gps, seq, 2 * HEAD_DIM), BF16)],
        compiler_params=_compiler_params(2),
        name="mixers",
    )(proj3, proj3, proj3, bias_row, proj3, proj3, w_spatial,
      b_spatial.reshape(n, SPATIAL, 1))


def _out_proj_kernel(a_ref, m_ref, x_ref, mod_ref, ga_ref, gg_ref, gf_ref, w_ref, x1_ref, h2_ref):
    a = a_ref[...].astype(F32)
    m = m_ref[...].astype(F32)
    mix = jnp.concatenate(
        [(a * _rms_scale(a) * ga_ref[...]).astype(BF16),
         (m * _rms_scale(m) * gg_ref[...]).astype(BF16)], axis=-1)
    y = jnp.dot(mix, w_ref[...], preferred_element_type=F32)
    gate_m = mod_ref[2:3, :]
    shift_f = mod_ref[3:4, :]
    scale_f = mod_ref[4:5, :]
    x1 = x_ref[...] + gate_m * y
    x1_ref[...] = x1
    h2 = x1 * _rms_scale(x1) * gf_ref[...]
    h2_ref[...] = (h2 * (1.0 + scale_f) + shift_f).astype(BF16)


def _out_proj(attn2, gm2, x2, mod3, ga, gg, gf, w_out, seq, tm=512):
    t, d = x2.shape
    wa = attn2.shape[1]
    wg = gm2.shape[1]
    steps_per_batch = seq // tm
    return pl.pallas_call(
        _out_proj_kernel,
        grid=(t // tm,),
        in_specs=[
            pl.BlockSpec((tm, wa), lambda i: (i, 0)),
            pl.BlockSpec((tm, wg), lambda i: (i, 0)),
            pl.BlockSpec((tm, d), lambda i: (i, 0)),
            pl.BlockSpec((None, N_MOD, d), lambda i: (i // steps_per_batch, 0, 0)),
            pl.BlockSpec((1, wa), lambda i: (0, 0)),
            pl.BlockSpec((1, wg), lambda i: (0, 0)),
            pl.BlockSpec((1, d), lambda i: (0, 0)),
            pl.BlockSpec((wa + wg, d), lambda i: (0, 0)),
        ],
        out_specs=[
            pl.BlockSpec((tm, d), lambda i: (i, 0)),
            pl.BlockSpec((tm, d), lambda i: (i, 0)),
        ],
        out_shape=[jax.ShapeDtypeStruct((t, d), F32), jax.ShapeDtypeStruct((t, d), BF16)],
        compiler_params=_compiler_params(1),
        name="out_proj",
    )(attn2, gm2, x2, mod3, ga, gg, gf, w_out)


def _ffn_kernel(h_ref, w1_ref, w2_ref, x1_hbm, mod_ref, o_ref, x1_buf, x1_sem):
    i = pl.program_id(0)
    j = pl.program_id(1)
    last = pl.num_programs(1) - 1
    tm = o_ref.shape[0]

    x1_copy = pltpu.make_async_copy(x1_hbm.at[pl.ds(i * tm, tm), :], x1_buf, x1_sem)

    def partial_out():
        a = jnp.dot(h_ref[...], w1_ref[...], preferred_element_type=F32)
        a = jnp.square(jnp.maximum(a, 0.0)).astype(BF16)
        return jnp.dot(a, w2_ref[...], preferred_element_type=F32)

    @pl.when(j == 0)
    def _():
        x1_copy.start()
        o_ref[...] = partial_out()

    @pl.when((j > 0) & (j < last))
    def _():
        o_ref[...] += partial_out()

    @pl.when(j == last)
    def _():
        gate_f = mod_ref[5:6, :]
        x1_copy.wait()
        o_ref[...] = x1_buf[...] + gate_f * (o_ref[...] + partial_out())


def _ffn(h2, w1, w2, x1, mod3, seq, tm=1024, tf=1024):
    t, d = h2.shape
    f = w1.shape[1]
    steps_per_batch = seq // tm
    assert f // tf >= 2, "first and last d_ff tiles must be distinct steps"
    return pl.pallas_call(
        _ffn_kernel,
        grid=(t // tm, f // tf),
        in_specs=[
            pl.BlockSpec((tm, d), lambda i, j: (i, 0)),
            pl.BlockSpec((d, tf), lambda i, j: (0, j)),
            pl.BlockSpec((tf, d), lambda i, j: (j, 0)),
            pl.BlockSpec(memory_space=pl.ANY),
            pl.BlockSpec((None, N_MOD, d), lambda i, j: (i // steps_per_batch, 0, 0)),
        ],
        out_specs=pl.BlockSpec((tm, d), lambda i, j: (i, 0)),
        out_shape=jax.ShapeDtypeStruct((t, d), F32),
        scratch_shapes=[pltpu.VMEM((tm, d), F32), pltpu.SemaphoreType.DMA(())],
        compiler_params=_compiler_params(2),
        name="ffn",
    )(h2, w1, w2, x1, mod3)


def kernel(x, c, w_ada, b_ada, mix_norm_g, w_in, q_norm_g, k_norm_g, rel_bias, gmlp_norm_g,
           w_spatial, b_spatial, attn_out_g, gmlp_out_g, w_out, ff_norm_g, w_ff1, w_ff2):
    b, seq, d = x.shape
    depth = w_ada.shape[0]
    n_heads = rel_bias.shape[1]
    n_groups = w_spatial.shape[1]
    attn_w = n_heads * HEAD_DIM
    gmlp_w = n_groups * HEAD_DIM
    assert n_heads == n_groups and w_in.shape[2] == 3 * attn_w + 2 * gmlp_w

    x2 = x.reshape(b * seq, d)
    for l in range(depth):
        mod3 = _ada_mod(c, w_ada[l], b_ada[l]).reshape(b, N_MOD, d)
        sec_gain = jnp.stack([
            jnp.tile(q_norm_g[l], n_heads), jnp.tile(k_norm_g[l], n_heads),
            gmlp_norm_g[l].reshape(-1)]).astype(F32)
        proj, (w_out_b, w_ff1_b, w_ff2_b) = _in_proj(
            x2, mod3, mix_norm_g[l].reshape(1, d), w_in[l], sec_gain, seq,
            side_weights=(w_out[l], w_ff1[l], w_ff2[l]))
        proj3 = proj.reshape(b, seq, -1)
        attn, gm = _mixers(proj3, _bias_row(rel_bias[l]), w_spatial[l], b_spatial[l])
        x1, h2 = _out_proj(
            attn.reshape(b * seq, attn_w), gm.reshape(b * seq, gmlp_w), x2, mod3,
            attn_out_g[l].reshape(1, attn_w), gmlp_out_g[l].reshape(1, gmlp_w),
            ff_norm_g[l].reshape(1, d), w_out_b, seq)
        x2 = _ffn(h2, w_ff1_b, w_ff2_b, x1, mod3, seq)
    return x2.reshape(b, seq, d)
```

```python
import functools

import jax
import jax.numpy as jnp
from jax import lax
from jax.experimental import pallas as pl
from jax.experimental.pallas import tpu as pltpu

CHUNK = 64
LEFT_CHUNKS = 8
HEAD_DIM = 128
MAX_REL = 128
SPATIAL = 128
N_MOD = 6
EPS = 1e-6
NEG_INF = -1e30
LOG2_E = 1.4426950408889634

Q_BLOCK = 2 * CHUNK
KEY_WINDOW = LEFT_CHUNKS * CHUNK + Q_BLOCK
LEFT_KEYS = LEFT_CHUNKS * CHUNK
ROLL_WIDTH = 1024
SCORE_LOOKAHEAD = 2

VMEM_LIMIT_BYTES = 56 * 1024 * 1024

ADA_ROWS = 256
IN_PROJ_ROWS = 256
W_IN_CHUNK_ROWS = 128
MIXER_GROUPS_PER_STEP = 4
OUT_PROJ_ROWS = 512
FFN_ROWS = 1024
FFN_COLS = 1024

BF16 = jnp.bfloat16
F32 = jnp.float32


def _compiler_params(n_grid_axes):
    return pltpu.CompilerParams(
        dimension_semantics=("arbitrary",) * n_grid_axes, vmem_limit_bytes=VMEM_LIMIT_BYTES)


def _rms_scale(t):
    return lax.rsqrt(jnp.mean(t * t, axis=-1, keepdims=True) + EPS)


def _gelu(t):
    return 0.5 * t * (1.0 + lax.erf(t * (0.5 ** 0.5)))


def _group_norm(t, gain, group):
    parts = []
    for s in range(0, t.shape[-1], group):
        p = t[:, s:s + group]
        parts.append(p * _rms_scale(p) * gain[:, s:s + group])
    return jnp.concatenate(parts, axis=-1)


def _ada_kernel(c_ref, w_ref, b_ref, o_ref):
    k = pl.program_id(0)
    c = c_ref[...]
    cond = (c * jax.nn.sigmoid(c)).astype(BF16)
    part = jnp.dot(cond, w_ref[...].astype(BF16), preferred_element_type=F32)

    @pl.when(k == 0)
    def _():
        o_ref[...] = part + b_ref[...]

    @pl.when(k > 0)
    def _():
        o_ref[...] += part


def _ada_mod(c, w_ada, b_ada, tk=ADA_ROWS):
    b, d = c.shape
    n = w_ada.shape[1]
    return pl.pallas_call(
        _ada_kernel,
        grid=(d // tk,),
        in_specs=[
            pl.BlockSpec((b, tk), lambda k: (0, k)),
            pl.BlockSpec((tk, n), lambda k: (k, 0)),
            pl.BlockSpec((1, n), lambda k: (0, 0)),
        ],
        out_specs=pl.BlockSpec((b, n), lambda k: (0, 0)),
        out_shape=jax.ShapeDtypeStruct((b, n), F32),
        compiler_params=_compiler_params(1),
        name="ada_mod",
    )(c, w_ada, b_ada.reshape(1, n))


def _in_proj_kernel(x_ref, mod_ref, g_ref, w_hbm, sg_ref, *rest, sec, n_side, w_chunk):
    side_in, o_ref, side_out = rest[:n_side], rest[n_side], rest[n_side + 1:2 * n_side + 1]
    w_ref, stage_ref, stage_sem = rest[2 * n_side + 1:]

    @pl.when(pl.program_id(0) == 0)
    def _():
        n_chunks = w_ref.shape[0] // w_chunk

        def chunk_copy(c):
            return pltpu.make_async_copy(
                w_hbm.at[pl.ds(c * w_chunk, w_chunk), :], stage_ref.at[c % 2], stage_sem.at[c % 2])

        chunk_copy(0).start()
        for c in range(n_chunks):
            if c + 1 < n_chunks:
                chunk_copy(c + 1).start()
            chunk_copy(c).wait()
            w_ref[c * w_chunk:(c + 1) * w_chunk, :] = stage_ref[c % 2].astype(BF16)

    for src, dst in zip(side_in, side_out):
        dst[...] = src[...].astype(BF16)

    x = x_ref[...]
    shift = mod_ref[0:1, :]
    scale = mod_ref[1:2, :]
    h = x * _rms_scale(x) * g_ref[...]
    h = (h * (1.0 + scale) + shift).astype(BF16)

    def section(s):
        return jnp.dot(h, w_ref[:, s * sec:(s + 1) * sec], preferred_element_type=F32)

    def put(s, val):
        o_ref[:, s * sec:(s + 1) * sec] = val.astype(BF16)

    put(4, _group_norm(_gelu(section(4)), sg_ref[2:3, :], HEAD_DIM))
    put(3, _gelu(section(3)))
    put(0, _group_norm(section(0), sg_ref[0:1, :], HEAD_DIM))
    put(1, _group_norm(section(1), sg_ref[1:2, :], HEAD_DIM))
    put(2, section(2))


def _in_proj(x2, mod3, g, w_in, sec_gain, seq, side_weights, tm=IN_PROJ_ROWS,
             w_chunk=W_IN_CHUNK_ROWS):
    t, d = x2.shape
    n = w_in.shape[1]
    sec = sec_gain.shape[-1]
    steps = t // tm
    steps_per_batch = seq // tm
    side_specs = [pl.BlockSpec((w.shape[0] // steps, w.shape[1]), lambda i: (i, 0))
                  for w in side_weights]
    outs = pl.pallas_call(
        functools.partial(_in_proj_kernel, sec=sec, n_side=len(side_weights), w_chunk=w_chunk),
        grid=(steps,),
        in_specs=[
            pl.BlockSpec((tm, d), lambda i: (i, 0)),
            pl.BlockSpec((None, N_MOD, d), lambda i: (i // steps_per_batch, 0, 0)),
            pl.BlockSpec((1, d), lambda i: (0, 0)),
            pl.BlockSpec(memory_space=pl.ANY),
            pl.BlockSpec(sec_gain.shape, lambda i: (0, 0)),
        ] + side_specs,
        out_specs=[pl.BlockSpec((tm, n), lambda i: (i, 0))] + side_specs,
        out_shape=[jax.ShapeDtypeStruct((t, n), BF16)]
        + [jax.ShapeDtypeStruct(w.shape, BF16) for w in side_weights],
        scratch_shapes=[pltpu.VMEM((d, n), BF16), pltpu.VMEM((2, w_chunk, n), F32),
                        pltpu.SemaphoreType.DMA((2,))],
        compiler_params=_compiler_params(1),
        name="in_proj",
    )(x2, mod3, g, w_in, sec_gain, *side_weights)
    return outs[0], outs[1:]


def _mixer_kernel(q_ref, k_ref, v_ref, base_ref, u_ref, z_ref, ws_ref, bs_ref,
                  o_ref, gm_ref, bias_ref, vext_ref, *, seq):
    @pl.when(pl.program_id(1) == 0)
    def _():
        for g in range(base_ref.shape[0]):
            _bias_table(base_ref.at[g], bias_ref.at[g])

    for g in range(base_ref.shape[0]):
        lanes = lambda ref: ref.at[:, g * HEAD_DIM:(g + 1) * HEAD_DIM]
        _attention(lanes(q_ref), lanes(k_ref), lanes(v_ref), lanes(o_ref),
                   bias_ref.at[g], vext_ref.at[g], seq)
    for g in range(base_ref.shape[0]):
        lanes = lambda ref: ref.at[:, g * HEAD_DIM:(g + 1) * HEAD_DIM]
        _spatial_gate(lanes(u_ref), lanes(z_ref), ws_ref.at[g], bs_ref.at[g], lanes(gm_ref), seq)


def _spatial_gate(u_ref, z_ref, w_ref, b_ref, o_ref, seq):
    t = lax.broadcasted_iota(jnp.int32, (SPATIAL, SPATIAL), 0)
    s = lax.broadcasted_iota(jnp.int32, (SPATIAL, SPATIAL), 1)
    w = jnp.where((t // CHUNK) >= (s // CHUNK), w_ref[...], 0.0).astype(BF16)
    bias = b_ref[...]
    for n in range(seq // SPATIAL):
        rows = slice(n * SPATIAL, (n + 1) * SPATIAL)
        y = jnp.dot(w, z_ref[rows, :], preferred_element_type=F32) + bias
        o_ref[rows, :] = (u_ref[rows, :].astype(F32) * y).astype(BF16)


def _bias_table(base_ref, bias_ref):
    tab = pltpu.roll(jnp.broadcast_to(base_ref[...], (Q_BLOCK, ROLL_WIDTH)), 0, 1,
                     stride=1, stride_axis=0)[:, :KEY_WINDOW]
    r = lax.broadcasted_iota(jnp.int32, (Q_BLOCK, KEY_WINDOW), 0)
    col = lax.broadcasted_iota(jnp.int32, (Q_BLOCK, KEY_WINDOW), 1)
    qc = r // CHUNK
    kc = col // CHUNK - LEFT_CHUNKS
    bias_ref[...] = jnp.where((kc <= qc) & (kc >= qc - LEFT_CHUNKS), tab * LOG2_E, NEG_INF)


def _attention(q_ref, k_ref, v_ref, o_ref, bias_ref, vext_ref, seq):
    scale = HEAD_DIM ** -0.5 * LOG2_E
    vext_ref[:, :HEAD_DIM] = v_ref[...]
    vext_ref[:, HEAD_DIM:] = jnp.ones((seq, HEAD_DIM), BF16)

    def window(qb):
        p0 = qb * Q_BLOCK
        ws = max(0, p0 - LEFT_KEYS)
        return p0, ws, ws - (p0 - LEFT_KEYS)

    def scores(qb):
        p0, ws, co = window(qb)
        s = lax.dot_general(q_ref[p0:p0 + Q_BLOCK, :], k_ref[ws:p0 + Q_BLOCK, :],
                            (((1,), (1,)), ((), ())), preferred_element_type=F32)
        return s * scale + bias_ref[:, co:KEY_WINDOW]

    def finish(qb, s):
        p0, ws, _ = window(qb)
        m = jnp.max(s, axis=-1, keepdims=True)
        p = jnp.exp2(s - m).astype(BF16)
        o = jnp.dot(p, vext_ref[ws:p0 + Q_BLOCK, :], preferred_element_type=F32)
        o_ref[p0:p0 + Q_BLOCK, :] = (o[:, :HEAD_DIM] / o[:, HEAD_DIM:]).astype(BF16)

    n_blocks = seq // Q_BLOCK
    pending = [scores(qb) for qb in range(SCORE_LOOKAHEAD)]
    for qb in range(n_blocks):
        if qb + SCORE_LOOKAHEAD < n_blocks:
            pending.append(scores(qb + SCORE_LOOKAHEAD))
        finish(qb, pending.pop(0))


def _bias_row(rel_bias):
    rb = rel_bias.astype(F32)
    h = rb.shape[0]
    far = rb[:, 2 * MAX_REL:]
    row = jnp.concatenate([
        jnp.broadcast_to(far, (h, LEFT_KEYS - MAX_REL + 1)),
        rb[:, 1:2 * MAX_REL][:, ::-1],
        jnp.broadcast_to(rb[:, :1], (h, KEY_WINDOW - LEFT_KEYS - MAX_REL)),
        jnp.broadcast_to(far, (h, ROLL_WIDTH - KEY_WINDOW)),
    ], axis=1)
    return row.reshape(h, 1, ROLL_WIDTH)


def _mixers(proj3, bias_row, w_spatial, b_spatial, groups_per_step=MIXER_GROUPS_PER_STEP):
    b, seq, _ = proj3.shape
    n = w_spatial.shape[0]
    gps = groups_per_step
    assert n % gps == 0
    width = gps * HEAD_DIM
    n_steps = n // gps
    blk = lambda sec: pl.BlockSpec((None, seq, width), lambda h, bi: (bi, 0, sec * n_steps + h))
    out_blk = pl.BlockSpec((None, seq, width), lambda h, bi: (bi, 0, h))
    out = jax.ShapeDtypeStruct((b, seq, n * HEAD_DIM), BF16)
    return pl.pallas_call(
        functools.partial(_mixer_kernel, seq=seq),
        grid=(n_steps, b),
        in_specs=[
            blk(0), blk(1), blk(2),
            pl.BlockSpec((gps, 1, ROLL_WIDTH), lambda h, bi: (h, 0, 0)),
            blk(3), blk(4),
            pl.BlockSpec((gps, SPATIAL, SPATIAL), lambda h, bi: (h, 0, 0)),
            pl.BlockSpec((gps, SPATIAL, 1), lambda h, bi: (h, 0, 0)),
        ],
        out_specs=[out_blk, out_blk],
        out_shape=[out, out],
        scratch_shapes=[pltpu.VMEM((gps, Q_BLOCK, KEY_WINDOW), F32),
                        pltpu.VMEM((gps, seq, 2 * HEAD_DIM), BF16)],
        compiler_params=_compiler_params(2),
        name="mixers",
    )(proj3, proj3, proj3, bias_row, proj3, proj3, w_spatial,
      b_spatial.reshape(n, SPATIAL, 1))


def _out_proj_kernel(a_ref, m_ref, x_ref, mod_ref, ga_ref, gg_ref, gf_ref, w_ref, x1_ref, h2_ref):
    a = a_ref[...].astype(F32)
    m = m_ref[...].astype(F32)
    mix = jnp.concatenate(
        [(a * _rms_scale(a) * ga_ref[...]).astype(BF16),
         (m * _rms_scale(m) * gg_ref[...]).astype(BF16)], axis=-1)
    y = jnp.dot(mix, w_ref[...], preferred_element_type=F32)
    gate_m = mod_ref[2:3, :]
    shift_f = mod_ref[3:4, :]
    scale_f = mod_ref[4:5, :]
    x1 = x_ref[...] + gate_m * y
    x1_ref[...] = x1
    h2 = x1 * _rms_scale(x1) * gf_ref[...]
    h2_ref[...] = (h2 * (1.0 + scale_f) + shift_f).astype(BF16)


def _out_proj(attn2, gm2, x2, mod3, ga, gg, gf, w_out, seq, tm=OUT_PROJ_ROWS):
    t, d = x2.shape
    wa = attn2.shape[1]
    wg = gm2.shape[1]
    steps_per_batch = seq // tm
    return pl.pallas_call(
        _out_proj_kernel,
        grid=(t // tm,),
        in_specs=[
            pl.BlockSpec((tm, wa), lambda i: (i, 0)),
            pl.BlockSpec((tm, wg), lambda i: (i, 0)),
            pl.BlockSpec((tm, d), lambda i: (i, 0)),
            pl.BlockSpec((None, N_MOD, d), lambda i: (i // steps_per_batch, 0, 0)),
            pl.BlockSpec((1, wa), lambda i: (0, 0)),
            pl.BlockSpec((1, wg), lambda i: (0, 0)),
            pl.BlockSpec((1, d), lambda i: (0, 0)),
            pl.BlockSpec((wa + wg, d), lambda i: (0, 0)),
        ],
        out_specs=[
            pl.BlockSpec((tm, d), lambda i: (i, 0)),
            pl.BlockSpec((tm, d), lambda i: (i, 0)),
        ],
        out_shape=[jax.ShapeDtypeStruct((t, d), F32), jax.ShapeDtypeStruct((t, d), BF16)],
        compiler_params=_compiler_params(1),
        name="out_proj",
    )(attn2, gm2, x2, mod3, ga, gg, gf, w_out)


def _ffn_kernel(h_ref, w1_ref, w2_ref, x1_hbm, mod_ref, o_ref, x1_buf, x1_sem):
    i = pl.program_id(0)
    j = pl.program_id(1)
    last = pl.num_programs(1) - 1
    tm = o_ref.shape[0]

    x1_copy = pltpu.make_async_copy(x1_hbm.at[pl.ds(i * tm, tm), :], x1_buf, x1_sem)

    def partial_out():
        a = jnp.dot(h_ref[...], w1_ref[...], preferred_element_type=F32)
        a = jnp.square(jnp.maximum(a, 0.0)).astype(BF16)
        return jnp.dot(a, w2_ref[...], preferred_element_type=F32)

    @pl.when(j == 0)
    def _():
        x1_copy.start()
        o_ref[...] = partial_out()

    @pl.when((j > 0) & (j < last))
    def _():
        o_ref[...] += partial_out()

    @pl.when(j == last)
    def _():
        gate_f = mod_ref[5:6, :]
        x1_copy.wait()
        o_ref[...] = x1_buf[...] + gate_f * (o_ref[...] + partial_out())


def _ffn(h2, w1, w2, x1, mod3, seq, tm=FFN_ROWS, tf=FFN_COLS):
    t, d = h2.shape
    f = w1.shape[1]
    steps_per_batch = seq // tm
    assert f // tf >= 2, "first and last d_ff tiles must be distinct steps"
    return pl.pallas_call(
        _ffn_kernel,
        grid=(t // tm, f // tf),
        in_specs=[
            pl.BlockSpec((tm, d), lambda i, j: (i, 0)),
            pl.BlockSpec((d, tf), lambda i, j: (0, j)),
            pl.BlockSpec((tf, d), lambda i, j: (j, 0)),
            pl.BlockSpec(memory_space=pl.ANY),
            pl.BlockSpec((None, N_MOD, d), lambda i, j: (i // steps_per_batch, 0, 0)),
        ],
        out_specs=pl.BlockSpec((tm, d), lambda i, j: (i, 0)),
        out_shape=jax.ShapeDtypeStruct((t, d), F32),
        scratch_shapes=[pltpu.VMEM((tm, d), F32), pltpu.SemaphoreType.DMA(())],
        compiler_params=_compiler_params(2),
        name="ffn",
    )(h2, w1, w2, x1, mod3)


def kernel(x, c, w_ada, b_ada, mix_norm_g, w_in, q_norm_g, k_norm_g, rel_bias, gmlp_norm_g,
           w_spatial, b_spatial, attn_out_g, gmlp_out_g, w_out, ff_norm_g, w_ff1, w_ff2):
    b, seq, d = x.shape
    depth = w_ada.shape[0]
    n_heads = rel_bias.shape[1]
    n_groups = w_spatial.shape[1]
    attn_w = n_heads * HEAD_DIM
    gmlp_w = n_groups * HEAD_DIM
    assert n_heads == n_groups and w_in.shape[2] == 3 * attn_w + 2 * gmlp_w

    x2 = x.reshape(b * seq, d)
    for l in range(depth):
        mod3 = _ada_mod(c, w_ada[l], b_ada[l]).reshape(b, N_MOD, d)
        sec_gain = jnp.stack([
            jnp.tile(q_norm_g[l], n_heads), jnp.tile(k_norm_g[l], n_heads),
            gmlp_norm_g[l].reshape(-1)]).astype(F32)
        proj, (w_out_b, w_ff1_b, w_ff2_b) = _in_proj(
            x2, mod3, mix_norm_g[l].reshape(1, d), w_in[l], sec_gain, seq,
            side_weights=(w_out[l], w_ff1[l], w_ff2[l]))
        proj3 = proj.reshape(b, seq, -1)
        attn, gm = _mixers(proj3, _bias_row(rel_bias[l]), w_spatial[l], b_spatial[l])
        x1, h2 = _out_proj(
            attn.reshape(b * seq, attn_w), gm.reshape(b * seq, gmlp_w), x2, mod3,
            attn_out_g[l].reshape(1, attn_w), gmlp_out_g[l].reshape(1, gmlp_w),
            ff_norm_g[l].reshape(1, d), w_out_b, seq)
        x2 = _ffn(h2, w_ff1_b, w_ff2_b, x1, mod3, seq)
    return x2.reshape(b, seq, d)
```

```python
import functools

import jax
import jax.numpy as jnp
from jax import lax
from jax.experimental import pallas as pl
from jax.experimental.pallas import tpu as pltpu

CHUNK = 64
LEFT_CHUNKS = 8
HEAD_DIM = 128
MAX_REL = 128
SPATIAL = 128
N_MOD = 6
EPS = 1e-6
NEG_INF = -1e30
LOG2_E = 1.4426950408889634

Q_BLOCK = 2 * CHUNK
KEY_WINDOW = LEFT_CHUNKS * CHUNK + Q_BLOCK
LEFT_KEYS = LEFT_CHUNKS * CHUNK
ROLL_WIDTH = 1024
SCORE_LOOKAHEAD = 2

VMEM_LIMIT_BYTES = 56 * 1024 * 1024

ADA_ROWS = 256
IN_PROJ_ROWS = 256
W_IN_CHUNK_ROWS = 128
MIXER_GROUPS_PER_STEP = 4
OUT_PROJ_ROWS = 512
FFN_ROWS = 1024
FFN_COLS = 1024

BF16 = jnp.bfloat16
F32 = jnp.float32


def _compiler_params(n_grid_axes):
    return pltpu.CompilerParams(
        dimension_semantics=("arbitrary",) * n_grid_axes, vmem_limit_bytes=VMEM_LIMIT_BYTES)


def _rms_scale(t):
    return lax.rsqrt(jnp.mean(t * t, axis=-1, keepdims=True) + EPS)


def _gelu(t):
    return 0.5 * t * (1.0 + lax.erf(t * (0.5 ** 0.5)))


def _group_norm(t, gain, group):
    parts = []
    for s in range(0, t.shape[-1], group):
        p = t[:, s:s + group]
        parts.append(p * _rms_scale(p) * gain[:, s:s + group])
    return jnp.concatenate(parts, axis=-1)


def _ada_kernel(c_ref, w_ref, b_ref, o_ref):
    k = pl.program_id(0)
    c = c_ref[...]
    cond = (c * jax.nn.sigmoid(c)).astype(BF16)
    part = jnp.dot(cond, w_ref[...].astype(BF16), preferred_element_type=F32)

    @pl.when(k == 0)
    def _():
        o_ref[...] = part + b_ref[...]

    @pl.when(k > 0)
    def _():
        o_ref[...] += part


def _ada_mod(c, w_ada, b_ada, tk=ADA_ROWS):
    b, d = c.shape
    n = w_ada.shape[1]
    return pl.pallas_call(
        _ada_kernel,
        grid=(d // tk,),
        in_specs=[
            pl.BlockSpec((b, tk), lambda k: (0, k)),
            pl.BlockSpec((tk, n), lambda k: (k, 0)),
            pl.BlockSpec((1, n), lambda k: (0, 0)),
        ],
        out_specs=pl.BlockSpec((b, n), lambda k: (0, 0)),
        out_shape=jax.ShapeDtypeStruct((b, n), F32),
        compiler_params=_compiler_params(1),
        name="ada_mod",
    )(c, w_ada, b_ada.reshape(1, n))


def _in_proj_kernel(x_ref, mod_ref, g_ref, w_hbm, sg_ref, *rest, sec, n_side, w_chunk):
    side_in, o_ref, side_out = rest[:n_side], rest[n_side], rest[n_side + 1:2 * n_side + 1]
    w_ref, stage_ref, stage_sem = rest[2 * n_side + 1:]

    @pl.when(pl.program_id(0) == 0)
    def _():
        n_chunks = w_ref.shape[0] // w_chunk

        def chunk_copy(c):
            return pltpu.make_async_copy(
                w_hbm.at[pl.ds(c * w_chunk, w_chunk), :], stage_ref.at[c % 2], stage_sem.at[c % 2])

        chunk_copy(0).start()
        for c in range(n_chunks):
            if c + 1 < n_chunks:
                chunk_copy(c + 1).start()
            chunk_copy(c).wait()
            w_ref[c * w_chunk:(c + 1) * w_chunk, :] = stage_ref[c % 2].astype(BF16)

    for src, dst in zip(side_in, side_out):
        dst[...] = src[...].astype(BF16)

    x = x_ref[...]
    shift = mod_ref[0:1, :]
    scale = mod_ref[1:2, :]
    h = x * _rms_scale(x) * g_ref[...]
    h = (h * (1.0 + scale) + shift).astype(BF16)

    def section(s):
        return jnp.dot(h, w_ref[:, s * sec:(s + 1) * sec], preferred_element_type=F32)

    def put(s, val):
        o_ref[:, s * sec:(s + 1) * sec] = val.astype(BF16)

    put(4, _group_norm(_gelu(section(4)), sg_ref[2:3, :], HEAD_DIM))
    put(3, _gelu(section(3)))
    put(0, _group_norm(section(0), sg_ref[0:1, :], HEAD_DIM))
    put(1, _group_norm(section(1), sg_ref[1:2, :], HEAD_DIM))
    put(2, section(2))


def _in_proj(x2, mod3, g, w_in, sec_gain, seq, side_weights, tm=IN_PROJ_ROWS,
             w_chunk=W_IN_CHUNK_ROWS):
    t, d = x2.shape
    n = w_in.shape[1]
    sec = sec_gain.shape[-1]
    steps = t // tm
    steps_per_batch = seq // tm
    side_specs = [pl.BlockSpec((w.shape[0] // steps, w.shape[1]), lambda i: (i, 0))
                  for w in side_weights]
    outs = pl.pallas_call(
        functools.partial(_in_proj_kernel, sec=sec, n_side=len(side_weights), w_chunk=w_chunk),
        grid=(steps,),
        in_specs=[
            pl.BlockSpec((tm, d), lambda i: (i, 0)),
            pl.BlockSpec((None, N_MOD, d), lambda i: (i // steps_per_batch, 0, 0)),
            pl.BlockSpec((1, d), lambda i: (0, 0)),
            pl.BlockSpec(memory_space=pl.ANY),
            pl.BlockSpec(sec_gain.shape, lambda i: (0, 0)),
        ] + side_specs,
        out_specs=[pl.BlockSpec((tm, n), lambda i: (i, 0))] + side_specs,
        out_shape=[jax.ShapeDtypeStruct((t, n), BF16)]
        + [jax.ShapeDtypeStruct(w.shape, BF16) for w in side_weights],
        scratch_shapes=[pltpu.VMEM((d, n), BF16), pltpu.VMEM((2, w_chunk, n), F32),
                        pltpu.SemaphoreType.DMA((2,))],
        compiler_params=_compiler_params(1),
        name="in_proj",
    )(x2, mod3, g, w_in, sec_gain, *side_weights)
    return outs[0], outs[1:]


def _mixer_kernel(q_ref, k_ref, v_ref, base_ref, u_ref, z_ref, ws_ref, bs_ref,
                  o_ref, gm_ref, bias_ref, vext_ref, kt_ref, *, seq):
    @pl.when(pl.program_id(1) == 0)
    def _():
        for g in range(base_ref.shape[0]):
            _bias_table(base_ref.at[g], bias_ref.at[g])

    for g in range(base_ref.shape[0]):
        lanes = lambda ref: ref.at[:, g * HEAD_DIM:(g + 1) * HEAD_DIM]
        _attention(lanes(q_ref), lanes(k_ref), lanes(v_ref), lanes(o_ref),
                   bias_ref.at[g], vext_ref.at[g], kt_ref.at[g], seq)
    for g in range(base_ref.shape[0]):
        lanes = lambda ref: ref.at[:, g * HEAD_DIM:(g + 1) * HEAD_DIM]
        _spatial_gate(lanes(u_ref), lanes(z_ref), ws_ref.at[g], bs_ref.at[g], lanes(gm_ref), seq)


def _spatial_gate(u_ref, z_ref, w_ref, b_ref, o_ref, seq):
    t = lax.broadcasted_iota(jnp.int32, (SPATIAL, SPATIAL), 0)
    s = lax.broadcasted_iota(jnp.int32, (SPATIAL, SPATIAL), 1)
    w = jnp.where((t // CHUNK) >= (s // CHUNK), w_ref[...], 0.0).astype(BF16)
    bias = b_ref[...]
    for n in range(seq // SPATIAL):
        rows = slice(n * SPATIAL, (n + 1) * SPATIAL)
        y = jnp.dot(w, z_ref[rows, :], preferred_element_type=F32) + bias
        o_ref[rows, :] = (u_ref[rows, :].astype(F32) * y).astype(BF16)


def _bias_table(base_ref, bias_ref):
    tab = pltpu.roll(jnp.broadcast_to(base_ref[...], (Q_BLOCK, ROLL_WIDTH)), 0, 1,
                     stride=1, stride_axis=0)[:, :KEY_WINDOW]
    r = lax.broadcasted_iota(jnp.int32, (Q_BLOCK, KEY_WINDOW), 0)
    col = lax.broadcasted_iota(jnp.int32, (Q_BLOCK, KEY_WINDOW), 1)
    qc = r // CHUNK
    kc = col // CHUNK - LEFT_CHUNKS
    bias_ref[...] = jnp.where((kc <= qc) & (kc >= qc - LEFT_CHUNKS), tab * LOG2_E, NEG_INF)


def _attention(q_ref, k_ref, v_ref, o_ref, bias_ref, vext_ref, kt_ref, seq):
    scale = HEAD_DIM ** -0.5 * LOG2_E
    vext_ref[:, :HEAD_DIM] = v_ref[...]
    vext_ref[:, HEAD_DIM:] = jnp.ones((seq, HEAD_DIM), BF16)
    kt_ref[...] = k_ref[...].T

    def window(qb):
        p0 = qb * Q_BLOCK
        ws = max(0, p0 - LEFT_KEYS)
        return p0, ws, ws - (p0 - LEFT_KEYS)

    def scores(qb):
        p0, ws, co = window(qb)
        s = jnp.dot(q_ref[p0:p0 + Q_BLOCK, :], kt_ref[:, ws:p0 + Q_BLOCK],
                    preferred_element_type=F32)
        return s * scale + bias_ref[:, co:KEY_WINDOW]

    def finish(qb, s):
        p0, ws, _ = window(qb)
        m = jnp.max(s, axis=-1, keepdims=True)
        p = jnp.exp2(s - m).astype(BF16)
        o = jnp.dot(p, vext_ref[ws:p0 + Q_BLOCK, :], preferred_element_type=F32)
        o_ref[p0:p0 + Q_BLOCK, :] = (o[:, :HEAD_DIM] / o[:, HEAD_DIM:]).astype(BF16)

    n_blocks = seq // Q_BLOCK
    pending = [scores(qb) for qb in range(SCORE_LOOKAHEAD)]
    for qb in range(n_blocks):
        if qb + SCORE_LOOKAHEAD < n_blocks:
            pending.append(scores(qb + SCORE_LOOKAHEAD))
        finish(qb, pending.pop(0))


def _bias_row(rel_bias):
    rb = rel_bias.astype(F32)
    h = rb.shape[0]
    far = rb[:, 2 * MAX_REL:]
    row = jnp.concatenate([
        jnp.broadcast_to(far, (h, LEFT_KEYS - MAX_REL + 1)),
        rb[:, 1:2 * MAX_REL][:, ::-1],
        jnp.broadcast_to(rb[:, :1], (h, KEY_WINDOW - LEFT_KEYS - MAX_REL)),
        jnp.broadcast_to(far, (h, ROLL_WIDTH - KEY_WINDOW)),
    ], axis=1)
    return row.reshape(h, 1, ROLL_WIDTH)


def _mixers(proj3, bias_row, w_spatial, b_spatial, groups_per_step=MIXER_GROUPS_PER_STEP):
    b, seq, _ = proj3.shape
    n = w_spatial.shape[0]
    gps = groups_per_step
    assert n % gps == 0
    width = gps * HEAD_DIM
    n_steps = n // gps
    blk = lambda sec: pl.BlockSpec((None, seq, width), lambda h, bi: (bi, 0, sec * n_steps + h))
    out_blk = pl.BlockSpec((None, seq, width), lambda h, bi: (bi, 0, h))
    out = jax.ShapeDtypeStruct((b, seq, n * HEAD_DIM), BF16)
    return pl.pallas_call(
        functools.partial(_mixer_kernel, seq=seq),
        grid=(n_steps, b),
        in_specs=[
            blk(0), blk(1), blk(2),
            pl.BlockSpec((gps, 1, ROLL_WIDTH), lambda h, bi: (h, 0, 0)),
            blk(3), blk(4),
            pl.BlockSpec((gps, SPATIAL, SPATIAL), lambda h, bi: (h, 0, 0)),
            pl.BlockSpec((gps, SPATIAL, 1), lambda h, bi: (h, 0, 0)),
        ],
        out_specs=[out_blk, out_blk],
        out_shape=[out, out],
        scratch_shapes=[pltpu.VMEM((gps, Q_BLOCK, KEY_WINDOW), F32),
                        pltpu.VMEM((gps, seq, 2 * HEAD_DIM), BF16),
                        pltpu.VMEM((gps, HEAD_DIM, seq), BF16)],
        compiler_params=_compiler_params(2),
        name="mixers",
    )(proj3, proj3, proj3, bias_row, proj3, proj3, w_spatial,
      b_spatial.reshape(n, SPATIAL, 1))


def _out_proj_kernel(a_ref, m_ref, x_ref, mod_ref, ga_ref, gg_ref, gf_ref, w_ref, x1_ref, h2_ref):
    a = a_ref[...].astype(F32)
    m = m_ref[...].astype(F32)
    mix = jnp.concatenate(
        [(a * _rms_scale(a) * ga_ref[...]).astype(BF16),
         (m * _rms_scale(m) * gg_ref[...]).astype(BF16)], axis=-1)
    y = jnp.dot(mix, w_ref[...], preferred_element_type=F32)
    gate_m = mod_ref[2:3, :]
    shift_f = mod_ref[3:4, :]
    scale_f = mod_ref[4:5, :]
    x1 = x_ref[...] + gate_m * y
    x1_ref[...] = x1
    h2 = x1 * _rms_scale(x1) * gf_ref[...]
    h2_ref[...] = (h2 * (1.0 + scale_f) + shift_f).astype(BF16)


def _out_proj(attn2, gm2, x2, mod3, ga, gg, gf, w_out, seq, tm=OUT_PROJ_ROWS):
    t, d = x2.shape
    wa = attn2.shape[1]
    wg = gm2.shape[1]
    steps_per_batch = seq // tm
    return pl.pallas_call(
        _out_proj_kernel,
        grid=(t // tm,),
        in_specs=[
            pl.BlockSpec((tm, wa), lambda i: (i, 0)),
            pl.BlockSpec((tm, wg), lambda i: (i, 0)),
            pl.BlockSpec((tm, d), lambda i: (i, 0)),
            pl.BlockSpec((None, N_MOD, d), lambda i: (i // steps_per_batch, 0, 0)),
            pl.BlockSpec((1, wa), lambda i: (0, 0)),
            pl.BlockSpec((1, wg), lambda i: (0, 0)),
            pl.BlockSpec((1, d), lambda i: (0, 0)),
            pl.BlockSpec((wa + wg, d), lambda i: (0, 0)),
        ],
        out_specs=[
            pl.BlockSpec((tm, d), lambda i: (i, 0)),
            pl.BlockSpec((tm, d), lambda i: (i, 0)),
        ],
        out_shape=[jax.ShapeDtypeStruct((t, d), F32), jax.ShapeDtypeStruct((t, d), BF16)],
        compiler_params=_compiler_params(1),
        name="out_proj",
    )(attn2, gm2, x2, mod3, ga, gg, gf, w_out)


def _ffn_kernel(h_ref, w1_ref, w2_ref, x1_hbm, mod_ref, o_ref, x1_buf, x1_sem):
    i = pl.program_id(0)
    j = pl.program_id(1)
    last = pl.num_programs(1) - 1
    tm = o_ref.shape[0]

    x1_copy = pltpu.make_async_copy(x1_hbm.at[pl.ds(i * tm, tm), :], x1_buf, x1_sem)

    def partial_out():
        a = jnp.dot(h_ref[...], w1_ref[...], preferred_element_type=F32)
        a = jnp.square(jnp.maximum(a, 0.0)).astype(BF16)
        return jnp.dot(a, w2_ref[...], preferred_element_type=F32)

    @pl.when(j == 0)
    def _():
        x1_copy.start()
        o_ref[...] = partial_out()

    @pl.when((j > 0) & (j < last))
    def _():
        o_ref[...] += partial_out()

    @pl.when(j == last)
    def _():
        gate_f = mod_ref[5:6, :]
        x1_copy.wait()
        o_ref[...] = x1_buf[...] + gate_f * (o_ref[...] + partial_out())


def _ffn(h2, w1, w2, x1, mod3, seq, tm=FFN_ROWS, tf=FFN_COLS):
    t, d = h2.shape
    f = w1.shape[1]
    steps_per_batch = seq // tm
    assert f // tf >= 2, "first and last d_ff tiles must be distinct steps"
    return pl.pallas_call(
        _ffn_kernel,
        grid=(t // tm, f // tf),
        in_specs=[
            pl.BlockSpec((tm, d), lambda i, j: (i, 0)),
            pl.BlockSpec((d, tf), lambda i, j: (0, j)),
            pl.BlockSpec((tf, d), lambda i, j: (j, 0)),
            pl.BlockSpec(memory_space=pl.ANY),
            pl.BlockSpec((None, N_MOD, d), lambda i, j: (i // steps_per_batch, 0, 0)),
        ],
        out_specs=pl.BlockSpec((tm, d), lambda i, j: (i, 0)),
        out_shape=jax.ShapeDtypeStruct((t, d), F32),
        scratch_shapes=[pltpu.VMEM((tm, d), F32), pltpu.SemaphoreType.DMA(())],
        compiler_params=_compiler_params(2),
        name="ffn",
    )(h2, w1, w2, x1, mod3)


def kernel(x, c, w_ada, b_ada, mix_norm_g, w_in, q_norm_g, k_norm_g, rel_bias, gmlp_norm_g,
           w_spatial, b_spatial, attn_out_g, gmlp_out_g, w_out, ff_norm_g, w_ff1, w_ff2):
    b, seq, d = x.shape
    depth = w_ada.shape[0]
    n_heads = rel_bias.shape[1]
    n_groups = w_spatial.shape[1]
    attn_w = n_heads * HEAD_DIM
    gmlp_w = n_groups * HEAD_DIM
    assert n_heads == n_groups and w_in.shape[2] == 3 * attn_w + 2 * gmlp_w

    x2 = x.reshape(b * seq, d)
    for l in range(depth):
        mod3 = _ada_mod(c, w_ada[l], b_ada[l]).reshape(b, N_MOD, d)
        sec_gain = jnp.stack([
            jnp.tile(q_norm_g[l], n_heads), jnp.tile(k_norm_g[l], n_heads),
            gmlp_norm_g[l].reshape(-1)]).astype(F32)
        proj, (w_out_b, w_ff1_b, w_ff2_b) = _in_proj(
            x2, mod3, mix_norm_g[l].reshape(1, d), w_in[l], sec_gain, seq,
            side_weights=(w_out[l], w_ff1[l], w_ff2[l]))
        proj3 = proj.reshape(b, seq, -1)
        attn, gm = _mixers(proj3, _bias_row(rel_bias[l]), w_spatial[l], b_spatial[l])
        x1, h2 = _out_proj(
            attn.reshape(b * seq, attn_w), gm.reshape(b * seq, gmlp_w), x2, mod3,
            attn_out_g[l].reshape(1, attn_w), gmlp_out_g[l].reshape(1, gmlp_w),
            ff_norm_g[l].reshape(1, d), w_out_b, seq)
        x2 = _ffn(h2, w_ff1_b, w_ff2_b, x1, mod3, seq)
    return x2.reshape(b, seq, d)
```

```python
import functools

import jax
import jax.numpy as jnp
from jax import lax
from jax.experimental import pallas as pl
from jax.experimental.pallas import tpu as pltpu

CHUNK = 64
LEFT_CHUNKS = 8
HEAD_DIM = 128
MAX_REL = 128
SPATIAL = 128
N_MOD = 6
EPS = 1e-6
NEG_INF = -1e30
LOG2_E = 1.4426950408889634

Q_BLOCK = 2 * CHUNK
KEY_WINDOW = LEFT_CHUNKS * CHUNK + Q_BLOCK
LEFT_KEYS = LEFT_CHUNKS * CHUNK
ROLL_WIDTH = 1024
SCORE_LOOKAHEAD = 2
MAX_SHIFT_GAP_LOG2 = 100.0

VMEM_LIMIT_BYTES = 56 * 1024 * 1024

ADA_ROWS = 256
IN_PROJ_ROWS = 256
W_IN_CHUNK_ROWS = 128
MIXER_GROUPS_PER_STEP = 4
OUT_PROJ_ROWS = 512
FFN_ROWS = 1024
FFN_COLS = 1024

BF16 = jnp.bfloat16
F32 = jnp.float32


def _compiler_params(n_grid_axes):
    return pltpu.CompilerParams(
        dimension_semantics=("arbitrary",) * n_grid_axes, vmem_limit_bytes=VMEM_LIMIT_BYTES)


def _rms_scale(t):
    return lax.rsqrt(jnp.mean(t * t, axis=-1, keepdims=True) + EPS)


def _gelu(t):
    return 0.5 * t * (1.0 + lax.erf(t * (0.5 ** 0.5)))


def _group_norm(t, gain, group):
    parts = []
    for s in range(0, t.shape[-1], group):
        p = t[:, s:s + group]
        parts.append(p * _rms_scale(p) * gain[:, s:s + group])
    return jnp.concatenate(parts, axis=-1)


def _ada_kernel(c_ref, w_ref, b_ref, o_ref):
    k = pl.program_id(0)
    c = c_ref[...]
    cond = (c * jax.nn.sigmoid(c)).astype(BF16)
    part = jnp.dot(cond, w_ref[...].astype(BF16), preferred_element_type=F32)

    @pl.when(k == 0)
    def _():
        o_ref[...] = part + b_ref[...]

    @pl.when(k > 0)
    def _():
        o_ref[...] += part


def _ada_mod(c, w_ada, b_ada, tk=ADA_ROWS):
    b, d = c.shape
    n = w_ada.shape[1]
    return pl.pallas_call(
        _ada_kernel,
        grid=(d // tk,),
        in_specs=[
            pl.BlockSpec((b, tk), lambda k: (0, k)),
            pl.BlockSpec((tk, n), lambda k: (k, 0)),
            pl.BlockSpec((1, n), lambda k: (0, 0)),
        ],
        out_specs=pl.BlockSpec((b, n), lambda k: (0, 0)),
        out_shape=jax.ShapeDtypeStruct((b, n), F32),
        compiler_params=_compiler_params(1),
        name="ada_mod",
    )(c, w_ada, b_ada.reshape(1, n))


def _in_proj_kernel(x_ref, mod_ref, g_ref, w_hbm, sg_ref, *rest, sec, n_side, w_chunk):
    side_in, o_ref, side_out = rest[:n_side], rest[n_side], rest[n_side + 1:2 * n_side + 1]
    w_ref, stage_ref, stage_sem = rest[2 * n_side + 1:]

    @pl.when(pl.program_id(0) == 0)
    def _():
        n_chunks = w_ref.shape[0] // w_chunk

        def chunk_copy(c):
            return pltpu.make_async_copy(
                w_hbm.at[pl.ds(c * w_chunk, w_chunk), :], stage_ref.at[c % 2], stage_sem.at[c % 2])

        chunk_copy(0).start()
        for c in range(n_chunks):
            if c + 1 < n_chunks:
                chunk_copy(c + 1).start()
            chunk_copy(c).wait()
            w_ref[c * w_chunk:(c + 1) * w_chunk, :] = stage_ref[c % 2].astype(BF16)

    for src, dst in zip(side_in, side_out):
        dst[...] = src[...].astype(BF16)

    x = x_ref[...]
    shift = mod_ref[0:1, :]
    scale = mod_ref[1:2, :]
    h = x * _rms_scale(x) * g_ref[...]
    h = (h * (1.0 + scale) + shift).astype(BF16)

    def section(s):
        return jnp.dot(h, w_ref[:, s * sec:(s + 1) * sec], preferred_element_type=F32)

    def put(s, val):
        o_ref[:, s * sec:(s + 1) * sec] = val.astype(BF16)

    put(4, _group_norm(_gelu(section(4)), sg_ref[2:3, :], HEAD_DIM))
    put(3, _gelu(section(3)))
    put(0, _group_norm(section(0), sg_ref[0:1, :], HEAD_DIM))
    put(1, _group_norm(section(1), sg_ref[1:2, :], HEAD_DIM))
    put(2, section(2))


def _in_proj(x2, mod3, g, w_in, sec_gain, seq, side_weights, tm=IN_PROJ_ROWS,
             w_chunk=W_IN_CHUNK_ROWS):
    t, d = x2.shape
    n = w_in.shape[1]
    sec = sec_gain.shape[-1]
    steps = t // tm
    steps_per_batch = seq // tm
    side_specs = [pl.BlockSpec((w.shape[0] // steps, w.shape[1]), lambda i: (i, 0))
                  for w in side_weights]
    outs = pl.pallas_call(
        functools.partial(_in_proj_kernel, sec=sec, n_side=len(side_weights), w_chunk=w_chunk),
        grid=(steps,),
        in_specs=[
            pl.BlockSpec((tm, d), lambda i: (i, 0)),
            pl.BlockSpec((None, N_MOD, d), lambda i: (i // steps_per_batch, 0, 0)),
            pl.BlockSpec((1, d), lambda i: (0, 0)),
            pl.BlockSpec(memory_space=pl.ANY),
            pl.BlockSpec(sec_gain.shape, lambda i: (0, 0)),
        ] + side_specs,
        out_specs=[pl.BlockSpec((tm, n), lambda i: (i, 0))] + side_specs,
        out_shape=[jax.ShapeDtypeStruct((t, n), BF16)]
        + [jax.ShapeDtypeStruct(w.shape, BF16) for w in side_weights],
        scratch_shapes=[pltpu.VMEM((d, n), BF16), pltpu.VMEM((2, w_chunk, n), F32),
                        pltpu.SemaphoreType.DMA((2,))],
        compiler_params=_compiler_params(1),
        name="in_proj",
    )(x2, mod3, g, w_in, sec_gain, *side_weights)
    return outs[0], outs[1:]


def _mixer_kernel(q_ref, k_ref, v_ref, base_ref, u_ref, z_ref, ws_ref, bs_ref,
                  o_ref, gm_ref, bias_ref, vext_ref, kt_ref, *, seq, exact_max):
    @pl.when(pl.program_id(1) == 0)
    def _():
        for g in range(base_ref.shape[0]):
            _bias_table(base_ref.at[g], bias_ref.at[g])

    for g in range(base_ref.shape[0]):
        lanes = lambda ref: ref.at[:, g * HEAD_DIM:(g + 1) * HEAD_DIM]
        _attention(lanes(q_ref), lanes(k_ref), lanes(v_ref), lanes(o_ref),
                   bias_ref.at[g], vext_ref.at[g], kt_ref.at[g], seq, exact_max)
    for g in range(base_ref.shape[0]):
        lanes = lambda ref: ref.at[:, g * HEAD_DIM:(g + 1) * HEAD_DIM]
        _spatial_gate(lanes(u_ref), lanes(z_ref), ws_ref.at[g], bs_ref.at[g], lanes(gm_ref), seq)


def _spatial_gate(u_ref, z_ref, w_ref, b_ref, o_ref, seq):
    t = lax.broadcasted_iota(jnp.int32, (SPATIAL, SPATIAL), 0)
    s = lax.broadcasted_iota(jnp.int32, (SPATIAL, SPATIAL), 1)
    w = jnp.where((t // CHUNK) >= (s // CHUNK), w_ref[...], 0.0).astype(BF16)
    bias = b_ref[...]
    for n in range(seq // SPATIAL):
        rows = slice(n * SPATIAL, (n + 1) * SPATIAL)
        y = jnp.dot(w, z_ref[rows, :], preferred_element_type=F32) + bias
        o_ref[rows, :] = (u_ref[rows, :].astype(F32) * y).astype(BF16)


def _bias_table(base_ref, bias_ref):
    tab = pltpu.roll(jnp.broadcast_to(base_ref[...], (Q_BLOCK, ROLL_WIDTH)), 0, 1,
                     stride=1, stride_axis=0)[:, :KEY_WINDOW]
    r = lax.broadcasted_iota(jnp.int32, (Q_BLOCK, KEY_WINDOW), 0)
    col = lax.broadcasted_iota(jnp.int32, (Q_BLOCK, KEY_WINDOW), 1)
    qc = r // CHUNK
    kc = col // CHUNK - LEFT_CHUNKS
    bias_ref[...] = jnp.where((kc <= qc) & (kc >= qc - LEFT_CHUNKS), tab * LOG2_E, NEG_INF)


def _attention(q_ref, k_ref, v_ref, o_ref, bias_ref, vext_ref, kt_ref, seq, exact_max):
    scale = HEAD_DIM ** -0.5 * LOG2_E
    vext_ref[:, :HEAD_DIM] = v_ref[...]
    vext_ref[:, HEAD_DIM:] = jnp.ones((seq, HEAD_DIM), BF16)
    kt_ref[...] = k_ref[...].T

    def window(qb):
        p0 = qb * Q_BLOCK
        ws = max(0, p0 - LEFT_KEYS)
        return p0, ws, ws - (p0 - LEFT_KEYS)

    def scores(qb):
        p0, ws, co = window(qb)
        s = jnp.dot(q_ref[p0:p0 + Q_BLOCK, :], kt_ref[:, ws:p0 + Q_BLOCK],
                    preferred_element_type=F32)
        return s * scale + bias_ref[:, co:KEY_WINDOW]

    def finish(qb, s):
        p0, ws, _ = window(qb)
        if exact_max:
            p = jnp.exp2(s - jnp.max(s, axis=-1, keepdims=True)).astype(BF16)
        else:
            p = jnp.exp2(s).astype(BF16)
        o = jnp.dot(p, vext_ref[ws:p0 + Q_BLOCK, :], preferred_element_type=F32)
        o_ref[p0:p0 + Q_BLOCK, :] = (o[:, :HEAD_DIM] / o[:, HEAD_DIM:]).astype(BF16)

    n_blocks = seq // Q_BLOCK
    pending = [scores(qb) for qb in range(SCORE_LOOKAHEAD)]
    for qb in range(n_blocks):
        if qb + SCORE_LOOKAHEAD < n_blocks:
            pending.append(scores(qb + SCORE_LOOKAHEAD))
        finish(qb, pending.pop(0))


def _bias_row(rel_bias):
    rb = rel_bias.astype(F32)
    h = rb.shape[0]
    far = rb[:, 2 * MAX_REL:]
    row = jnp.concatenate([
        jnp.broadcast_to(far, (h, LEFT_KEYS - MAX_REL + 1)),
        rb[:, 1:2 * MAX_REL][:, ::-1],
        jnp.broadcast_to(rb[:, :1], (h, KEY_WINDOW - LEFT_KEYS - MAX_REL)),
        jnp.broadcast_to(far, (h, ROLL_WIDTH - KEY_WINDOW)),
    ], axis=1)
    return row.reshape(h, 1, ROLL_WIDTH)


def _score_bound(q_gain, k_gain, rel_bias):
    dot_bound = (1.01 * HEAD_DIM ** 0.5) * jnp.max(jnp.abs(q_gain)) * jnp.max(jnp.abs(k_gain))
    b_max = jnp.max(rel_bias)
    return dot_bound + b_max, 2.0 * dot_bound + (b_max - jnp.min(rel_bias))


def _mixers(proj3, bias_row, w_spatial, b_spatial, exact_max,
            groups_per_step=MIXER_GROUPS_PER_STEP):
    b, seq, _ = proj3.shape
    n = w_spatial.shape[0]
    gps = groups_per_step
    assert n % gps == 0
    width = gps * HEAD_DIM
    n_steps = n // gps
    blk = lambda sec: pl.BlockSpec((None, seq, width), lambda h, bi: (bi, 0, sec * n_steps + h))
    out_blk = pl.BlockSpec((None, seq, width), lambda h, bi: (bi, 0, h))
    out = jax.ShapeDtypeStruct((b, seq, n * HEAD_DIM), BF16)
    return pl.pallas_call(
        functools.partial(_mixer_kernel, seq=seq, exact_max=exact_max),
        grid=(n_steps, b),
        in_specs=[
            blk(0), blk(1), blk(2),
            pl.BlockSpec((gps, 1, ROLL_WIDTH), lambda h, bi: (h, 0, 0)),
            blk(3), blk(4),
            pl.BlockSpec((gps, SPATIAL, SPATIAL), lambda h, bi: (h, 0, 0)),
            pl.BlockSpec((gps, SPATIAL, 1), lambda h, bi: (h, 0, 0)),
        ],
        out_specs=[out_blk, out_blk],
        out_shape=[out, out],
        scratch_shapes=[pltpu.VMEM((gps, Q_BLOCK, KEY_WINDOW), F32),
                        pltpu.VMEM((gps, seq, 2 * HEAD_DIM), BF16),
                        pltpu.VMEM((gps, HEAD_DIM, seq), BF16)],
        compiler_params=_compiler_params(2),
        name="mixers",
    )(proj3, proj3, proj3, bias_row, proj3, proj3, w_spatial,
      b_spatial.reshape(n, SPATIAL, 1))


def _out_proj_kernel(a_ref, m_ref, x_ref, mod_ref, ga_ref, gg_ref, gf_ref, w_ref, x1_ref, h2_ref):
    a = a_ref[...].astype(F32)
    m = m_ref[...].astype(F32)
    mix = jnp.concatenate(
        [(a * _rms_scale(a) * ga_ref[...]).astype(BF16),
         (m * _rms_scale(m) * gg_ref[...]).astype(BF16)], axis=-1)
    y = jnp.dot(mix, w_ref[...], preferred_element_type=F32)
    gate_m = mod_ref[2:3, :]
    shift_f = mod_ref[3:4, :]
    scale_f = mod_ref[4:5, :]
    x1 = x_ref[...] + gate_m * y
    x1_ref[...] = x1
    h2 = x1 * _rms_scale(x1) * gf_ref[...]
    h2_ref[...] = (h2 * (1.0 + scale_f) + shift_f).astype(BF16)


def _out_proj(attn2, gm2, x2, mod3, ga, gg, gf, w_out, seq, tm=OUT_PROJ_ROWS):
    t, d = x2.shape
    wa = attn2.shape[1]
    wg = gm2.shape[1]
    steps_per_batch = seq // tm
    return pl.pallas_call(
        _out_proj_kernel,
        grid=(t // tm,),
        in_specs=[
            pl.BlockSpec((tm, wa), lambda i: (i, 0)),
            pl.BlockSpec((tm, wg), lambda i: (i, 0)),
            pl.BlockSpec((tm, d), lambda i: (i, 0)),
            pl.BlockSpec((None, N_MOD, d), lambda i: (i // steps_per_batch, 0, 0)),
            pl.BlockSpec((1, wa), lambda i: (0, 0)),
            pl.BlockSpec((1, wg), lambda i: (0, 0)),
            pl.BlockSpec((1, d), lambda i: (0, 0)),
            pl.BlockSpec((wa + wg, d), lambda i: (0, 0)),
        ],
        out_specs=[
            pl.BlockSpec((tm, d), lambda i: (i, 0)),
            pl.BlockSpec((tm, d), lambda i: (i, 0)),
        ],
        out_shape=[jax.ShapeDtypeStruct((t, d), F32), jax.ShapeDtypeStruct((t, d), BF16)],
        compiler_params=_compiler_params(1),
        name="out_proj",
    )(attn2, gm2, x2, mod3, ga, gg, gf, w_out)


def _ffn_kernel(h_ref, w1_ref, w2_ref, x1_hbm, mod_ref, o_ref, x1_buf, x1_sem):
    i = pl.program_id(0)
    j = pl.program_id(1)
    last = pl.num_programs(1) - 1
    tm = o_ref.shape[0]

    x1_copy = pltpu.make_async_copy(x1_hbm.at[pl.ds(i * tm, tm), :], x1_buf, x1_sem)

    def partial_out():
        a = jnp.dot(h_ref[...], w1_ref[...], preferred_element_type=F32)
        a = jnp.square(jnp.maximum(a, 0.0)).astype(BF16)
        return jnp.dot(a, w2_ref[...], preferred_element_type=F32)

    @pl.when(j == 0)
    def _():
        x1_copy.start()
        o_ref[...] = partial_out()

    @pl.when((j > 0) & (j < last))
    def _():
        o_ref[...] += partial_out()

    @pl.when(j == last)
    def _():
        gate_f = mod_ref[5:6, :]
        x1_copy.wait()
        o_ref[...] = x1_buf[...] + gate_f * (o_ref[...] + partial_out())


def _ffn(h2, w1, w2, x1, mod3, seq, tm=FFN_ROWS, tf=FFN_COLS):
    t, d = h2.shape
    f = w1.shape[1]
    steps_per_batch = seq // tm
    assert f // tf >= 2, "first and last d_ff tiles must be distinct steps"
    return pl.pallas_call(
        _ffn_kernel,
        grid=(t // tm, f // tf),
        in_specs=[
            pl.BlockSpec((tm, d), lambda i, j: (i, 0)),
            pl.BlockSpec((d, tf), lambda i, j: (0, j)),
            pl.BlockSpec((tf, d), lambda i, j: (j, 0)),
            pl.BlockSpec(memory_space=pl.ANY),
            pl.BlockSpec((None, N_MOD, d), lambda i, j: (i // steps_per_batch, 0, 0)),
        ],
        out_specs=pl.BlockSpec((tm, d), lambda i, j: (i, 0)),
        out_shape=jax.ShapeDtypeStruct((t, d), F32),
        scratch_shapes=[pltpu.VMEM((tm, d), F32), pltpu.SemaphoreType.DMA(())],
        compiler_params=_compiler_params(2),
        name="ffn",
    )(h2, w1, w2, x1, mod3)


def kernel(x, c, w_ada, b_ada, mix_norm_g, w_in, q_norm_g, k_norm_g, rel_bias, gmlp_norm_g,
           w_spatial, b_spatial, attn_out_g, gmlp_out_g, w_out, ff_norm_g, w_ff1, w_ff2):
    b, seq, d = x.shape
    depth = w_ada.shape[0]
    n_heads = rel_bias.shape[1]
    n_groups = w_spatial.shape[1]
    attn_w = n_heads * HEAD_DIM
    gmlp_w = n_groups * HEAD_DIM
    assert n_heads == n_groups and w_in.shape[2] == 3 * attn_w + 2 * gmlp_w

    x2 = x.reshape(b * seq, d)
    for l in range(depth):
        mod3 = _ada_mod(c, w_ada[l], b_ada[l]).reshape(b, N_MOD, d)
        sec_gain = jnp.stack([
            jnp.tile(q_norm_g[l], n_heads), jnp.tile(k_norm_g[l], n_heads),
            gmlp_norm_g[l].reshape(-1)]).astype(F32)
        proj, (w_out_b, w_ff1_b, w_ff2_b) = _in_proj(
            x2, mod3, mix_norm_g[l].reshape(1, d), w_in[l], sec_gain, seq,
            side_weights=(w_out[l], w_ff1[l], w_ff2[l]))
        proj3 = proj.reshape(b, seq, -1)
        bias_row = _bias_row(rel_bias[l])
        bound, gap = _score_bound(q_norm_g[l], k_norm_g[l], rel_bias[l])
        attn, gm = lax.cond(
            gap * LOG2_E < MAX_SHIFT_GAP_LOG2,
            lambda: _mixers(proj3, bias_row - bound, w_spatial[l], b_spatial[l], exact_max=False),
            lambda: _mixers(proj3, bias_row, w_spatial[l], b_spatial[l], exact_max=True))
        x1, h2 = _out_proj(
            attn.reshape(b * seq, attn_w), gm.reshape(b * seq, gmlp_w), x2, mod3,
            attn_out_g[l].reshape(1, attn_w), gmlp_out_g[l].reshape(1, gmlp_w),
            ff_norm_g[l].reshape(1, d), w_out_b, seq)
        x2 = _ffn(h2, w_ff1_b, w_ff2_b, x1, mod3, seq)
    return x2.reshape(b, seq, d)
```

```python
import functools

import jax
import jax.numpy as jnp
from jax import lax
from jax.experimental import pallas as pl
from jax.experimental.pallas import tpu as pltpu

CHUNK = 64
LEFT_CHUNKS = 8
HEAD_DIM = 128
MAX_REL = 128
SPATIAL = 128
N_MOD = 6
EPS = 1e-6
NEG_INF = -1e30
LOG2_E = 1.4426950408889634

Q_BLOCK = 2 * CHUNK
KEY_WINDOW = LEFT_CHUNKS * CHUNK + Q_BLOCK
LEFT_KEYS = LEFT_CHUNKS * CHUNK
ROLL_WIDTH = 1024
SCORE_LOOKAHEAD = 2
MAX_SHIFT_GAP_LOG2 = 100.0

VMEM_LIMIT_BYTES = 56 * 1024 * 1024

ADA_ROWS = 256
IN_PROJ_ROWS = 256
W_IN_CHUNK_ROWS = 128
MIXER_GROUPS_PER_STEP = 4
OUT_PROJ_ROWS = 512
FFN_ROWS = 1024
FFN_COLS = 1024

BF16 = jnp.bfloat16
F32 = jnp.float32


def _compiler_params(n_grid_axes):
    return pltpu.CompilerParams(
        dimension_semantics=("arbitrary",) * n_grid_axes, vmem_limit_bytes=VMEM_LIMIT_BYTES)


def _rms_scale(t):
    return lax.rsqrt(jnp.mean(t * t, axis=-1, keepdims=True) + EPS)


def _gelu(t):
    return 0.5 * t * (1.0 + lax.erf(t * (0.5 ** 0.5)))


def _group_norm(t, gain, group):
    parts = []
    for s in range(0, t.shape[-1], group):
        p = t[:, s:s + group]
        parts.append(p * _rms_scale(p) * gain[:, s:s + group])
    return jnp.concatenate(parts, axis=-1)


def _ada_kernel(c_ref, w_ref, b_ref, o_ref):
    k = pl.program_id(0)
    c = c_ref[...]
    cond = (c * jax.nn.sigmoid(c)).astype(BF16)
    part = jnp.dot(cond, w_ref[...].astype(BF16), preferred_element_type=F32)

    @pl.when(k == 0)
    def _():
        o_ref[...] = part + b_ref[...]

    @pl.when(k > 0)
    def _():
        o_ref[...] += part


def _ada_mod(c, w_ada, b_ada, tk=ADA_ROWS):
    b, d = c.shape
    n = w_ada.shape[1]
    return pl.pallas_call(
        _ada_kernel,
        grid=(d // tk,),
        in_specs=[
            pl.BlockSpec((b, tk), lambda k: (0, k)),
            pl.BlockSpec((tk, n), lambda k: (k, 0)),
            pl.BlockSpec((1, n), lambda k: (0, 0)),
        ],
        out_specs=pl.BlockSpec((b, n), lambda k: (0, 0)),
        out_shape=jax.ShapeDtypeStruct((b, n), F32),
        compiler_params=_compiler_params(1),
        name="ada_mod",
    )(c, w_ada, b_ada.reshape(1, n))


def _in_proj_kernel(x_ref, mod_ref, g_ref, w_hbm, sg_ref, *rest, sec, n_side, w_chunk):
    side_in, o_ref, side_out = rest[:n_side], rest[n_side], rest[n_side + 1:2 * n_side + 1]
    w_ref, stage_ref, stage_sem = rest[2 * n_side + 1:]

    @pl.when(pl.program_id(0) == 0)
    def _():
        n_chunks = w_ref.shape[0] // w_chunk

        def chunk_copy(c):
            return pltpu.make_async_copy(
                w_hbm.at[pl.ds(c * w_chunk, w_chunk), :], stage_ref.at[c % 2], stage_sem.at[c % 2])

        chunk_copy(0).start()
        for c in range(n_chunks):
            if c + 1 < n_chunks:
                chunk_copy(c + 1).start()
            chunk_copy(c).wait()
            w_ref[c * w_chunk:(c + 1) * w_chunk, :] = stage_ref[c % 2].astype(BF16)

    for src, dst in zip(side_in, side_out):
        dst[...] = src[...].astype(BF16)

    x = x_ref[...]
    shift = mod_ref[0:1, :]
    scale = mod_ref[1:2, :]
    h = x * _rms_scale(x) * g_ref[...]
    h = (h * (1.0 + scale) + shift).astype(BF16)

    def section(s):
        return jnp.dot(h, w_ref[:, s * sec:(s + 1) * sec], preferred_element_type=F32)

    def put(s, val):
        o_ref[:, s * sec:(s + 1) * sec] = val.astype(BF16)

    put(4, _group_norm(_gelu(section(4)), sg_ref[2:3, :], HEAD_DIM))
    put(3, _gelu(section(3)))
    put(0, _group_norm(section(0), sg_ref[0:1, :], HEAD_DIM))
    put(1, _group_norm(section(1), sg_ref[1:2, :], HEAD_DIM))
    put(2, section(2))


def _in_proj(x2, mod3, g, w_in, sec_gain, seq, side_weights, tm=IN_PROJ_ROWS,
             w_chunk=W_IN_CHUNK_ROWS):
    t, d = x2.shape
    n = w_in.shape[1]
    sec = sec_gain.shape[-1]
    steps = t // tm
    steps_per_batch = seq // tm
    side_specs = [pl.BlockSpec((w.shape[0] // steps, w.shape[1]), lambda i: (i, 0))
                  for w in side_weights]
    outs = pl.pallas_call(
        functools.partial(_in_proj_kernel, sec=sec, n_side=len(side_weights), w_chunk=w_chunk),
        grid=(steps,),
        in_specs=[
            pl.BlockSpec((tm, d), lambda i: (i, 0)),
            pl.BlockSpec((None, N_MOD, d), lambda i: (i // steps_per_batch, 0, 0)),
            pl.BlockSpec((1, d), lambda i: (0, 0)),
            pl.BlockSpec(memory_space=pl.ANY),
            pl.BlockSpec(sec_gain.shape, lambda i: (0, 0)),
        ] + side_specs,
        out_specs=[pl.BlockSpec((tm, n), lambda i: (i, 0))] + side_specs,
        out_shape=[jax.ShapeDtypeStruct((t, n), BF16)]
        + [jax.ShapeDtypeStruct(w.shape, BF16) for w in side_weights],
        scratch_shapes=[pltpu.VMEM((d, n), BF16), pltpu.VMEM((2, w_chunk, n), F32),
                        pltpu.SemaphoreType.DMA((2,))],
        compiler_params=_compiler_params(1),
        name="in_proj",
    )(x2, mod3, g, w_in, sec_gain, *side_weights)
    return outs[0], outs[1:]


def _mixer_kernel(shifted_ref, q_ref, k_ref, v_ref, base_ref, u_ref, z_ref, ws_ref, bs_ref,
                  o_ref, gm_ref, bias_ref, vext_ref, kt_ref, *, seq):
    @pl.when(pl.program_id(1) == 0)
    def _():
        for g in range(base_ref.shape[0]):
            _bias_table(base_ref.at[g], bias_ref.at[g])

    def body(exact_max):
        for g in range(base_ref.shape[0]):
            lanes = lambda ref: ref.at[:, g * HEAD_DIM:(g + 1) * HEAD_DIM]
            _attention(lanes(q_ref), lanes(k_ref), lanes(v_ref), lanes(o_ref),
                       bias_ref.at[g], vext_ref.at[g], kt_ref.at[g], seq, exact_max)
        for g in range(base_ref.shape[0]):
            lanes = lambda ref: ref.at[:, g * HEAD_DIM:(g + 1) * HEAD_DIM]
            _spatial_gate(lanes(u_ref), lanes(z_ref), ws_ref.at[g], bs_ref.at[g], lanes(gm_ref),
                          seq)

    @pl.when(shifted_ref[0] == 1)
    def _():
        body(exact_max=False)

    @pl.when(shifted_ref[0] != 1)
    def _():
        body(exact_max=True)


def _spatial_gate(u_ref, z_ref, w_ref, b_ref, o_ref, seq):
    t = lax.broadcasted_iota(jnp.int32, (SPATIAL, SPATIAL), 0)
    s = lax.broadcasted_iota(jnp.int32, (SPATIAL, SPATIAL), 1)
    w = jnp.where((t // CHUNK) >= (s // CHUNK), w_ref[...], 0.0).astype(BF16)
    bias = b_ref[...]
    for n in range(seq // SPATIAL):
        rows = slice(n * SPATIAL, (n + 1) * SPATIAL)
        y = jnp.dot(w, z_ref[rows, :], preferred_element_type=F32) + bias
        o_ref[rows, :] = (u_ref[rows, :].astype(F32) * y).astype(BF16)


def _bias_table(base_ref, bias_ref):
    tab = pltpu.roll(jnp.broadcast_to(base_ref[...], (Q_BLOCK, ROLL_WIDTH)), 0, 1,
                     stride=1, stride_axis=0)[:, :KEY_WINDOW]
    r = lax.broadcasted_iota(jnp.int32, (Q_BLOCK, KEY_WINDOW), 0)
    col = lax.broadcasted_iota(jnp.int32, (Q_BLOCK, KEY_WINDOW), 1)
    qc = r // CHUNK
    kc = col // CHUNK - LEFT_CHUNKS
    bias_ref[...] = jnp.where((kc <= qc) & (kc >= qc - LEFT_CHUNKS), tab * LOG2_E, NEG_INF)


def _attention(q_ref, k_ref, v_ref, o_ref, bias_ref, vext_ref, kt_ref, seq, exact_max):
    scale = HEAD_DIM ** -0.5 * LOG2_E
    vext_ref[:, :HEAD_DIM] = v_ref[...]
    vext_ref[:, HEAD_DIM:] = jnp.ones((seq, HEAD_DIM), BF16)
    kt_ref[...] = k_ref[...].T

    def window(qb):
        p0 = qb * Q_BLOCK
        ws = max(0, p0 - LEFT_KEYS)
        return p0, ws, ws - (p0 - LEFT_KEYS)

    def scores(qb):
        p0, ws, co = window(qb)
        s = jnp.dot(q_ref[p0:p0 + Q_BLOCK, :], kt_ref[:, ws:p0 + Q_BLOCK],
                    preferred_element_type=F32)
        return s * scale + bias_ref[:, co:KEY_WINDOW]

    def finish(qb, s):
        p0, ws, _ = window(qb)
        if exact_max:
            p = jnp.exp2(s - jnp.max(s, axis=-1, keepdims=True)).astype(BF16)
        else:
            p = jnp.exp2(s).astype(BF16)
        o = jnp.dot(p, vext_ref[ws:p0 + Q_BLOCK, :], preferred_element_type=F32)
        o_ref[p0:p0 + Q_BLOCK, :] = (o[:, :HEAD_DIM] / o[:, HEAD_DIM:]).astype(BF16)

    n_blocks = seq // Q_BLOCK
    pending = [scores(qb) for qb in range(SCORE_LOOKAHEAD)]
    for qb in range(n_blocks):
        if qb + SCORE_LOOKAHEAD < n_blocks:
            pending.append(scores(qb + SCORE_LOOKAHEAD))
        finish(qb, pending.pop(0))


def _bias_row(rel_bias):
    rb = rel_bias.astype(F32)
    h = rb.shape[0]
    far = rb[:, 2 * MAX_REL:]
    row = jnp.concatenate([
        jnp.broadcast_to(far, (h, LEFT_KEYS - MAX_REL + 1)),
        rb[:, 1:2 * MAX_REL][:, ::-1],
        jnp.broadcast_to(rb[:, :1], (h, KEY_WINDOW - LEFT_KEYS - MAX_REL)),
        jnp.broadcast_to(far, (h, ROLL_WIDTH - KEY_WINDOW)),
    ], axis=1)
    return row.reshape(h, 1, ROLL_WIDTH)


def _score_bound(q_gain, k_gain, rel_bias):
    dot_bound = (1.01 * HEAD_DIM ** 0.5) * jnp.max(jnp.abs(q_gain)) * jnp.max(jnp.abs(k_gain))
    b_max = jnp.max(rel_bias)
    return dot_bound + b_max, 2.0 * dot_bound + (b_max - jnp.min(rel_bias))


def _mixers(proj3, bias_row, shifted, w_spatial, b_spatial, groups_per_step=MIXER_GROUPS_PER_STEP):
    b, seq, _ = proj3.shape
    n = w_spatial.shape[0]
    gps = groups_per_step
    assert n % gps == 0
    width = gps * HEAD_DIM
    n_steps = n // gps
    blk = lambda sec: pl.BlockSpec((None, seq, width), lambda h, bi: (bi, 0, sec * n_steps + h))
    out_blk = pl.BlockSpec((None, seq, width), lambda h, bi: (bi, 0, h))
    out = jax.ShapeDtypeStruct((b, seq, n * HEAD_DIM), BF16)
    return pl.pallas_call(
        functools.partial(_mixer_kernel, seq=seq),
        grid=(n_steps, b),
        in_specs=[
            pl.BlockSpec(memory_space=pltpu.SMEM),
            blk(0), blk(1), blk(2),
            pl.BlockSpec((gps, 1, ROLL_WIDTH), lambda h, bi: (h, 0, 0)),
            blk(3), blk(4),
            pl.BlockSpec((gps, SPATIAL, SPATIAL), lambda h, bi: (h, 0, 0)),
            pl.BlockSpec((gps, SPATIAL, 1), lambda h, bi: (h, 0, 0)),
        ],
        out_specs=[out_blk, out_blk],
        out_shape=[out, out],
        scratch_shapes=[pltpu.VMEM((gps, Q_BLOCK, KEY_WINDOW), F32),
                        pltpu.VMEM((gps, seq, 2 * HEAD_DIM), BF16),
                        pltpu.VMEM((gps, HEAD_DIM, seq), BF16)],
        compiler_params=_compiler_params(2),
        name="mixers",
    )(shifted, proj3, proj3, proj3, bias_row, proj3, proj3, w_spatial,
      b_spatial.reshape(n, SPATIAL, 1))


def _out_proj_kernel(a_ref, m_ref, x_ref, mod_ref, ga_ref, gg_ref, gf_ref, w_ref, x1_ref, h2_ref):
    a = a_ref[...].astype(F32)
    m = m_ref[...].astype(F32)
    mix = jnp.concatenate(
        [(a * _rms_scale(a) * ga_ref[...]).astype(BF16),
         (m * _rms_scale(m) * gg_ref[...]).astype(BF16)], axis=-1)
    y = jnp.dot(mix, w_ref[...], preferred_element_type=F32)
    gate_m = mod_ref[2:3, :]
    shift_f = mod_ref[3:4, :]
    scale_f = mod_ref[4:5, :]
    x1 = x_ref[...] + gate_m * y
    x1_ref[...] = x1
    h2 = x1 * _rms_scale(x1) * gf_ref[...]
    h2_ref[...] = (h2 * (1.0 + scale_f) + shift_f).astype(BF16)


def _out_proj(attn2, gm2, x2, mod3, ga, gg, gf, w_out, seq, tm=OUT_PROJ_ROWS):
    t, d = x2.shape
    wa = attn2.shape[1]
    wg = gm2.shape[1]
    steps_per_batch = seq // tm
    return pl.pallas_call(
        _out_proj_kernel,
        grid=(t // tm,),
        in_specs=[
            pl.BlockSpec((tm, wa), lambda i: (i, 0)),
            pl.BlockSpec((tm, wg), lambda i: (i, 0)),
            pl.BlockSpec((tm, d), lambda i: (i, 0)),
            pl.BlockSpec((None, N_MOD, d), lambda i: (i // steps_per_batch, 0, 0)),
            pl.BlockSpec((1, wa), lambda i: (0, 0)),
            pl.BlockSpec((1, wg), lambda i: (0, 0)),
            pl.BlockSpec((1, d), lambda i: (0, 0)),
            pl.BlockSpec((wa + wg, d), lambda i: (0, 0)),
        ],
        out_specs=[
            pl.BlockSpec((tm, d), lambda i: (i, 0)),
            pl.BlockSpec((tm, d), lambda i: (i, 0)),
        ],
        out_shape=[jax.ShapeDtypeStruct((t, d), F32), jax.ShapeDtypeStruct((t, d), BF16)],
        compiler_params=_compiler_params(1),
        name="out_proj",
    )(attn2, gm2, x2, mod3, ga, gg, gf, w_out)


def _ffn_kernel(h_ref, w1_ref, w2_ref, x1_hbm, mod_ref, o_ref, x1_buf, x1_sem):
    i = pl.program_id(0)
    j = pl.program_id(1)
    last = pl.num_programs(1) - 1
    tm = o_ref.shape[0]

    x1_copy = pltpu.make_async_copy(x1_hbm.at[pl.ds(i * tm, tm), :], x1_buf, x1_sem)

    def partial_out():
        a = jnp.dot(h_ref[...], w1_ref[...], preferred_element_type=F32)
        a = jnp.square(jnp.maximum(a, 0.0)).astype(BF16)
        return jnp.dot(a, w2_ref[...], preferred_element_type=F32)

    @pl.when(j == 0)
    def _():
        x1_copy.start()
        o_ref[...] = partial_out()

    @pl.when((j > 0) & (j < last))
    def _():
        o_ref[...] += partial_out()

    @pl.when(j == last)
    def _():
        gate_f = mod_ref[5:6, :]
        x1_copy.wait()
        o_ref[...] = x1_buf[...] + gate_f * (o_ref[...] + partial_out())


def _ffn(h2, w1, w2, x1, mod3, seq, tm=FFN_ROWS, tf=FFN_COLS):
    t, d = h2.shape
    f = w1.shape[1]
    steps_per_batch = seq // tm
    assert f // tf >= 2, "first and last d_ff tiles must be distinct steps"
    return pl.pallas_call(
        _ffn_kernel,
        grid=(t // tm, f // tf),
        in_specs=[
            pl.BlockSpec((tm, d), lambda i, j: (i, 0)),
            pl.BlockSpec((d, tf), lambda i, j: (0, j)),
            pl.BlockSpec((tf, d), lambda i, j: (j, 0)),
            pl.BlockSpec(memory_space=pl.ANY),
            pl.BlockSpec((None, N_MOD, d), lambda i, j: (i // steps_per_batch, 0, 0)),
        ],
        out_specs=pl.BlockSpec((tm, d), lambda i, j: (i, 0)),
        out_shape=jax.ShapeDtypeStruct((t, d), F32),
        scratch_shapes=[pltpu.VMEM((tm, d), F32), pltpu.SemaphoreType.DMA(())],
        compiler_params=_compiler_params(2),
        name="ffn",
    )(h2, w1, w2, x1, mod3)


def kernel(x, c, w_ada, b_ada, mix_norm_g, w_in, q_norm_g, k_norm_g, rel_bias, gmlp_norm_g,
           w_spatial, b_spatial, attn_out_g, gmlp_out_g, w_out, ff_norm_g, w_ff1, w_ff2):
    b, seq, d = x.shape
    depth = w_ada.shape[0]
    n_heads = rel_bias.shape[1]
    n_groups = w_spatial.shape[1]
    attn_w = n_heads * HEAD_DIM
    gmlp_w = n_groups * HEAD_DIM
    assert n_heads == n_groups and w_in.shape[2] == 3 * attn_w + 2 * gmlp_w

    x2 = x.reshape(b * seq, d)
    for l in range(depth):
        mod3 = _ada_mod(c, w_ada[l], b_ada[l]).reshape(b, N_MOD, d)
        sec_gain = jnp.stack([
            jnp.tile(q_norm_g[l], n_heads), jnp.tile(k_norm_g[l], n_heads),
            gmlp_norm_g[l].reshape(-1)]).astype(F32)
        proj, (w_out_b, w_ff1_b, w_ff2_b) = _in_proj(
            x2, mod3, mix_norm_g[l].reshape(1, d), w_in[l], sec_gain, seq,
            side_weights=(w_out[l], w_ff1[l], w_ff2[l]))
        proj3 = proj.reshape(b, seq, -1)
        bound, gap = _score_bound(q_norm_g[l], k_norm_g[l], rel_bias[l])
        shifted = gap * LOG2_E < MAX_SHIFT_GAP_LOG2
        bias_row = _bias_row(rel_bias[l]) - jnp.where(shifted, bound, 0.0)
        attn, gm = _mixers(proj3, bias_row, shifted.astype(jnp.int32).reshape(1),
                           w_spatial[l], b_spatial[l])
        x1, h2 = _out_proj(
            attn.reshape(b * seq, attn_w), gm.reshape(b * seq, gmlp_w), x2, mod3,
            attn_out_g[l].reshape(1, attn_w), gmlp_out_g[l].reshape(1, gmlp_w),
            ff_norm_g[l].reshape(1, d), w_out_b, seq)
        x2 = _ffn(h2, w_ff1_b, w_ff2_b, x1, mod3, seq)
    return x2.reshape(b, seq, d)
```

```python
import functools

import jax
import jax.numpy as jnp
from jax import lax
from jax.experimental import pallas as pl
from jax.experimental.pallas import tpu as pltpu

CHUNK = 64
LEFT_CHUNKS = 8
HEAD_DIM = 128
MAX_REL = 128
SPATIAL = 128
N_MOD = 6
EPS = 1e-6
NEG_INF = -1e30
LOG2_E = 1.4426950408889634

Q_BLOCK = 2 * CHUNK
KEY_WINDOW = LEFT_CHUNKS * CHUNK + Q_BLOCK
LEFT_KEYS = LEFT_CHUNKS * CHUNK
ROLL_WIDTH = 1024
SCORE_LOOKAHEAD = 2
MAX_SHIFT_GAP_LOG2 = 100.0

VMEM_LIMIT_BYTES = 56 * 1024 * 1024
FFN_VMEM_LIMIT_BYTES = 60 * 1024 * 1024

ADA_ROWS = 256
IN_PROJ_ROWS = 256
W_IN_CHUNK_ROWS = 128
MIXER_GROUPS_PER_STEP = 4
OUT_PROJ_ROWS = 512
FFN_ROWS = 1024
FFN_COLS = 1024

BF16 = jnp.bfloat16
F32 = jnp.float32


def _compiler_params(n_grid_axes, vmem_limit_bytes=VMEM_LIMIT_BYTES):
    return pltpu.CompilerParams(
        dimension_semantics=("arbitrary",) * n_grid_axes, vmem_limit_bytes=vmem_limit_bytes)


def _rms_scale(t):
    return lax.rsqrt(jnp.mean(t * t, axis=-1, keepdims=True) + EPS)


def _gelu(t):
    return 0.5 * t * (1.0 + lax.erf(t * (0.5 ** 0.5)))


def _group_norm(t, gain, group):
    parts = []
    for s in range(0, t.shape[-1], group):
        p = t[:, s:s + group]
        parts.append(p * _rms_scale(p) * gain[:, s:s + group])
    return jnp.concatenate(parts, axis=-1)


def _ada_kernel(c_ref, w_ref, b_ref, o_ref):
    k = pl.program_id(0)
    c = c_ref[...]
    cond = (c * jax.nn.sigmoid(c)).astype(BF16)
    part = jnp.dot(cond, w_ref[...].astype(BF16), preferred_element_type=F32)

    @pl.when(k == 0)
    def _():
        o_ref[...] = part + b_ref[...]

    @pl.when(k > 0)
    def _():
        o_ref[...] += part


def _ada_mod(c, w_ada, b_ada, tk=ADA_ROWS):
    b, d = c.shape
    n = w_ada.shape[1]
    return pl.pallas_call(
        _ada_kernel,
        grid=(d // tk,),
        in_specs=[
            pl.BlockSpec((b, tk), lambda k: (0, k)),
            pl.BlockSpec((tk, n), lambda k: (k, 0)),
            pl.BlockSpec((1, n), lambda k: (0, 0)),
        ],
        out_specs=pl.BlockSpec((b, n), lambda k: (0, 0)),
        out_shape=jax.ShapeDtypeStruct((b, n), F32),
        compiler_params=_compiler_params(1),
        name="ada_mod",
    )(c, w_ada, b_ada.reshape(1, n))


def _in_proj_kernel(x_ref, mod_ref, g_ref, w_hbm, sg_ref, *rest, sec, n_side, w_chunk):
    side_in, o_ref, side_out = rest[:n_side], rest[n_side], rest[n_side + 1:2 * n_side + 1]
    w_ref, stage_ref, stage_sem = rest[2 * n_side + 1:]

    @pl.when(pl.program_id(0) == 0)
    def _():
        n_chunks = w_ref.shape[0] // w_chunk

        def chunk_copy(c):
            return pltpu.make_async_copy(
                w_hbm.at[pl.ds(c * w_chunk, w_chunk), :], stage_ref.at[c % 2], stage_sem.at[c % 2])

        chunk_copy(0).start()
        for c in range(n_chunks):
            if c + 1 < n_chunks:
                chunk_copy(c + 1).start()
            chunk_copy(c).wait()
            w_ref[c * w_chunk:(c + 1) * w_chunk, :] = stage_ref[c % 2].astype(BF16)

    for src, dst in zip(side_in, side_out):
        dst[...] = src[...].astype(BF16)

    x = x_ref[...]
    shift = mod_ref[0:1, :]
    scale = mod_ref[1:2, :]
    h = x * _rms_scale(x) * g_ref[...]
    h = (h * (1.0 + scale) + shift).astype(BF16)

    def section(s):
        return jnp.dot(h, w_ref[:, s * sec:(s + 1) * sec], preferred_element_type=F32)

    def put(s, val):
        o_ref[:, s * sec:(s + 1) * sec] = val.astype(BF16)

    put(4, _group_norm(_gelu(section(4)), sg_ref[2:3, :], HEAD_DIM))
    put(3, _gelu(section(3)))
    put(0, _group_norm(section(0), sg_ref[0:1, :], HEAD_DIM))
    put(1, _group_norm(section(1), sg_ref[1:2, :], HEAD_DIM))
    put(2, section(2))


def _in_proj(x2, mod3, g, w_in, sec_gain, seq, side_weights, tm=IN_PROJ_ROWS,
             w_chunk=W_IN_CHUNK_ROWS):
    t, d = x2.shape
    n = w_in.shape[1]
    sec = sec_gain.shape[-1]
    steps = t // tm
    steps_per_batch = seq // tm
    side_specs = [pl.BlockSpec((w.shape[0] // steps, w.shape[1]), lambda i: (i, 0))
                  for w in side_weights]
    outs = pl.pallas_call(
        functools.partial(_in_proj_kernel, sec=sec, n_side=len(side_weights), w_chunk=w_chunk),
        grid=(steps,),
        in_specs=[
            pl.BlockSpec((tm, d), lambda i: (i, 0)),
            pl.BlockSpec((None, N_MOD, d), lambda i: (i // steps_per_batch, 0, 0)),
            pl.BlockSpec((1, d), lambda i: (0, 0)),
            pl.BlockSpec(memory_space=pl.ANY),
            pl.BlockSpec(sec_gain.shape, lambda i: (0, 0)),
        ] + side_specs,
        out_specs=[pl.BlockSpec((tm, n), lambda i: (i, 0))] + side_specs,
        out_shape=[jax.ShapeDtypeStruct((t, n), BF16)]
        + [jax.ShapeDtypeStruct(w.shape, BF16) for w in side_weights],
        scratch_shapes=[pltpu.VMEM((d, n), BF16), pltpu.VMEM((2, w_chunk, n), F32),
                        pltpu.SemaphoreType.DMA((2,))],
        compiler_params=_compiler_params(1),
        name="in_proj",
    )(x2, mod3, g, w_in, sec_gain, *side_weights)
    return outs[0], outs[1:]


def _mixer_kernel(shifted_ref, q_ref, k_ref, v_ref, base_ref, u_ref, z_ref, ws_ref, bs_ref,
                  o_ref, gm_ref, bias_ref, vext_ref, kt_ref, *, seq):
    @pl.when(pl.program_id(1) == 0)
    def _():
        for g in range(base_ref.shape[0]):
            _bias_table(base_ref.at[g], bias_ref.at[g])

    def body(exact_max):
        for g in range(base_ref.shape[0]):
            lanes = lambda ref: ref.at[:, g * HEAD_DIM:(g + 1) * HEAD_DIM]
            _attention(lanes(q_ref), lanes(k_ref), lanes(v_ref), lanes(o_ref),
                       bias_ref.at[g], vext_ref.at[g], kt_ref.at[g], seq, exact_max)
        for g in range(base_ref.shape[0]):
            lanes = lambda ref: ref.at[:, g * HEAD_DIM:(g + 1) * HEAD_DIM]
            _spatial_gate(lanes(u_ref), lanes(z_ref), ws_ref.at[g], bs_ref.at[g], lanes(gm_ref),
                          seq)

    @pl.when(shifted_ref[0] == 1)
    def _():
        body(exact_max=False)

    @pl.when(shifted_ref[0] != 1)
    def _():
        body(exact_max=True)


def _spatial_gate(u_ref, z_ref, w_ref, b_ref, o_ref, seq):
    t = lax.broadcasted_iota(jnp.int32, (SPATIAL, SPATIAL), 0)
    s = lax.broadcasted_iota(jnp.int32, (SPATIAL, SPATIAL), 1)
    w = jnp.where((t // CHUNK) >= (s // CHUNK), w_ref[...], 0.0).astype(BF16)
    bias = b_ref[...]
    for n in range(seq // SPATIAL):
        rows = slice(n * SPATIAL, (n + 1) * SPATIAL)
        y = jnp.dot(w, z_ref[rows, :], preferred_element_type=F32) + bias
        o_ref[rows, :] = (u_ref[rows, :].astype(F32) * y).astype(BF16)


def _bias_table(base_ref, bias_ref):
    tab = pltpu.roll(jnp.broadcast_to(base_ref[...], (Q_BLOCK, ROLL_WIDTH)), 0, 1,
                     stride=1, stride_axis=0)[:, :KEY_WINDOW]
    r = lax.broadcasted_iota(jnp.int32, (Q_BLOCK, KEY_WINDOW), 0)
    col = lax.broadcasted_iota(jnp.int32, (Q_BLOCK, KEY_WINDOW), 1)
    qc = r // CHUNK
    kc = col // CHUNK - LEFT_CHUNKS
    bias_ref[...] = jnp.where((kc <= qc) & (kc >= qc - LEFT_CHUNKS), tab * LOG2_E, NEG_INF)


def _attention(q_ref, k_ref, v_ref, o_ref, bias_ref, vext_ref, kt_ref, seq, exact_max):
    scale = HEAD_DIM ** -0.5 * LOG2_E
    vext_ref[:, :HEAD_DIM] = v_ref[...]
    vext_ref[:, HEAD_DIM:] = jnp.ones((seq, HEAD_DIM), BF16)
    kt_ref[...] = k_ref[...].T

    def window(qb):
        p0 = qb * Q_BLOCK
        ws = max(0, p0 - LEFT_KEYS)
        return p0, ws, ws - (p0 - LEFT_KEYS)

    def scores(qb):
        p0, ws, co = window(qb)
        s = jnp.dot(q_ref[p0:p0 + Q_BLOCK, :], kt_ref[:, ws:p0 + Q_BLOCK],
                    preferred_element_type=F32)
        return s * scale + bias_ref[:, co:KEY_WINDOW]

    def finish(qb, s):
        p0, ws, _ = window(qb)
        if exact_max:
            p = jnp.exp2(s - jnp.max(s, axis=-1, keepdims=True)).astype(BF16)
        else:
            p = jnp.exp2(s).astype(BF16)
        o = jnp.dot(p, vext_ref[ws:p0 + Q_BLOCK, :], preferred_element_type=F32)
        o_ref[p0:p0 + Q_BLOCK, :] = (o[:, :HEAD_DIM] / o[:, HEAD_DIM:]).astype(BF16)

    n_blocks = seq // Q_BLOCK
    pending = [scores(qb) for qb in range(SCORE_LOOKAHEAD)]
    for qb in range(n_blocks):
        if qb + SCORE_LOOKAHEAD < n_blocks:
            pending.append(scores(qb + SCORE_LOOKAHEAD))
        finish(qb, pending.pop(0))


def _bias_row(rel_bias):
    rb = rel_bias.astype(F32)
    h = rb.shape[0]
    far = rb[:, 2 * MAX_REL:]
    row = jnp.concatenate([
        jnp.broadcast_to(far, (h, LEFT_KEYS - MAX_REL + 1)),
        rb[:, 1:2 * MAX_REL][:, ::-1],
        jnp.broadcast_to(rb[:, :1], (h, KEY_WINDOW - LEFT_KEYS - MAX_REL)),
        jnp.broadcast_to(far, (h, ROLL_WIDTH - KEY_WINDOW)),
    ], axis=1)
    return row.reshape(h, 1, ROLL_WIDTH)


def _score_bound(q_gain, k_gain, rel_bias):
    dot_bound = (1.01 * HEAD_DIM ** 0.5) * jnp.max(jnp.abs(q_gain)) * jnp.max(jnp.abs(k_gain))
    b_max = jnp.max(rel_bias)
    return dot_bound + b_max, 2.0 * dot_bound + (b_max - jnp.min(rel_bias))


def _mixers(proj3, bias_row, shifted, w_spatial, b_spatial, groups_per_step=MIXER_GROUPS_PER_STEP):
    b, seq, _ = proj3.shape
    n = w_spatial.shape[0]
    gps = groups_per_step
    assert n % gps == 0
    width = gps * HEAD_DIM
    n_steps = n // gps
    blk = lambda sec: pl.BlockSpec((None, seq, width), lambda h, bi: (bi, 0, sec * n_steps + h))
    out_blk = pl.BlockSpec((None, seq, width), lambda h, bi: (bi, 0, h))
    out = jax.ShapeDtypeStruct((b, seq, n * HEAD_DIM), BF16)
    return pl.pallas_call(
        functools.partial(_mixer_kernel, seq=seq),
        grid=(n_steps, b),
        in_specs=[
            pl.BlockSpec(memory_space=pltpu.SMEM),
            blk(0), blk(1), blk(2),
            pl.BlockSpec((gps, 1, ROLL_WIDTH), lambda h, bi: (h, 0, 0)),
            blk(3), blk(4),
            pl.BlockSpec((gps, SPATIAL, SPATIAL), lambda h, bi: (h, 0, 0)),
            pl.BlockSpec((gps, SPATIAL, 1), lambda h, bi: (h, 0, 0)),
        ],
        out_specs=[out_blk, out_blk],
        out_shape=[out, out],
        scratch_shapes=[pltpu.VMEM((gps, Q_BLOCK, KEY_WINDOW), F32),
                        pltpu.VMEM((gps, seq, 2 * HEAD_DIM), BF16),
                        pltpu.VMEM((gps, HEAD_DIM, seq), BF16)],
        compiler_params=_compiler_params(2),
        name="mixers",
    )(shifted, proj3, proj3, proj3, bias_row, proj3, proj3, w_spatial,
      b_spatial.reshape(n, SPATIAL, 1))


def _out_proj_kernel(a_ref, m_ref, x_ref, mod_ref, ga_ref, gg_ref, gf_ref, w_ref, x1_ref, h2_ref):
    a = a_ref[...].astype(F32)
    m = m_ref[...].astype(F32)
    mix = jnp.concatenate(
        [(a * _rms_scale(a) * ga_ref[...]).astype(BF16),
         (m * _rms_scale(m) * gg_ref[...]).astype(BF16)], axis=-1)
    y = jnp.dot(mix, w_ref[...], preferred_element_type=F32)
    gate_m = mod_ref[2:3, :]
    shift_f = mod_ref[3:4, :]
    scale_f = mod_ref[4:5, :]
    x1 = x_ref[...] + gate_m * y
    x1_ref[...] = x1
    h2 = x1 * _rms_scale(x1) * gf_ref[...]
    h2_ref[...] = (h2 * (1.0 + scale_f) + shift_f).astype(BF16)


def _out_proj(attn2, gm2, x2, mod3, ga, gg, gf, w_out, seq, tm=OUT_PROJ_ROWS):
    t, d = x2.shape
    wa = attn2.shape[1]
    wg = gm2.shape[1]
    steps_per_batch = seq // tm
    return pl.pallas_call(
        _out_proj_kernel,
        grid=(t // tm,),
        in_specs=[
            pl.BlockSpec((tm, wa), lambda i: (i, 0)),
            pl.BlockSpec((tm, wg), lambda i: (i, 0)),
            pl.BlockSpec((tm, d), lambda i: (i, 0)),
            pl.BlockSpec((None, N_MOD, d), lambda i: (i // steps_per_batch, 0, 0)),
            pl.BlockSpec((1, wa), lambda i: (0, 0)),
            pl.BlockSpec((1, wg), lambda i: (0, 0)),
            pl.BlockSpec((1, d), lambda i: (0, 0)),
            pl.BlockSpec((wa + wg, d), lambda i: (0, 0)),
        ],
        out_specs=[
            pl.BlockSpec((tm, d), lambda i: (i, 0)),
            pl.BlockSpec((tm, d), lambda i: (i, 0)),
        ],
        out_shape=[jax.ShapeDtypeStruct((t, d), F32), jax.ShapeDtypeStruct((t, d), BF16)],
        compiler_params=_compiler_params(1),
        name="out_proj",
    )(attn2, gm2, x2, mod3, ga, gg, gf, w_out)


def _ffn_kernel(h_ref, w1_hbm, w2_hbm, x1_hbm, mod_ref, o_ref,
                w1_buf, w2_buf, w_sem, x1_buf, x1_sem, *, tf):
    i = pl.program_id(0)
    tm = o_ref.shape[0]
    n_tiles = w1_hbm.shape[1] // tf

    def weight_copies(j, slot):
        return (pltpu.make_async_copy(w1_hbm.at[:, pl.ds(j * tf, tf)], w1_buf.at[slot],
                                      w_sem.at[0, slot]),
                pltpu.make_async_copy(w2_hbm.at[pl.ds(j * tf, tf), :], w2_buf.at[slot],
                                      w_sem.at[1, slot]))

    def start(j, slot):
        for cp in weight_copies(j, slot):
            cp.start()

    def wait(j, slot):
        for cp in weight_copies(j, slot):
            cp.wait()

    @pl.when(i == 0)
    def _():
        start(0, 0)

    x1_copy = pltpu.make_async_copy(x1_hbm.at[pl.ds(i * tm, tm), :], x1_buf, x1_sem)
    x1_copy.start()

    for j in range(n_tiles):
        slot = j % 2
        if j + 1 < n_tiles:
            start(j + 1, 1 - slot)
        else:
            @pl.when(i + 1 < pl.num_programs(0))
            def _():
                start(0, 1 - slot)
        wait(j, slot)
        a = jnp.dot(h_ref[...], w1_buf[slot], preferred_element_type=F32)
        a = jnp.square(jnp.maximum(a, 0.0)).astype(BF16)
        y = jnp.dot(a, w2_buf[slot], preferred_element_type=F32)
        if j == 0:
            o_ref[...] = y
        elif j + 1 < n_tiles:
            o_ref[...] += y
        else:
            x1_copy.wait()
            o_ref[...] = x1_buf[...] + mod_ref[5:6, :] * (o_ref[...] + y)


def _ffn(h2, w1, w2, x1, mod3, seq, tm=FFN_ROWS, tf=FFN_COLS):
    t, d = h2.shape
    f = w1.shape[1]
    steps_per_batch = seq // tm
    n_tiles = f // tf
    assert n_tiles >= 2 and n_tiles % 2 == 0
    return pl.pallas_call(
        functools.partial(_ffn_kernel, tf=tf),
        grid=(t // tm,),
        in_specs=[
            pl.BlockSpec((tm, d), lambda i: (i, 0)),
            pl.BlockSpec(memory_space=pl.ANY),
            pl.BlockSpec(memory_space=pl.ANY),
            pl.BlockSpec(memory_space=pl.ANY),
            pl.BlockSpec((None, N_MOD, d), lambda i: (i // steps_per_batch, 0, 0)),
        ],
        out_specs=pl.BlockSpec((tm, d), lambda i: (i, 0)),
        out_shape=jax.ShapeDtypeStruct((t, d), F32),
        scratch_shapes=[pltpu.VMEM((2, d, tf), BF16), pltpu.VMEM((2, tf, d), BF16),
                        pltpu.SemaphoreType.DMA((2, 2)),
                        pltpu.VMEM((tm, d), F32), pltpu.SemaphoreType.DMA(())],
        compiler_params=_compiler_params(1, FFN_VMEM_LIMIT_BYTES),
        name="ffn",
    )(h2, w1, w2, x1, mod3)


def kernel(x, c, w_ada, b_ada, mix_norm_g, w_in, q_norm_g, k_norm_g, rel_bias, gmlp_norm_g,
           w_spatial, b_spatial, attn_out_g, gmlp_out_g, w_out, ff_norm_g, w_ff1, w_ff2):
    b, seq, d = x.shape
    depth = w_ada.shape[0]
    n_heads = rel_bias.shape[1]
    n_groups = w_spatial.shape[1]
    attn_w = n_heads * HEAD_DIM
    gmlp_w = n_groups * HEAD_DIM
    assert n_heads == n_groups and w_in.shape[2] == 3 * attn_w + 2 * gmlp_w

    x2 = x.reshape(b * seq, d)
    for l in range(depth):
        mod3 = _ada_mod(c, w_ada[l], b_ada[l]).reshape(b, N_MOD, d)
        sec_gain = jnp.stack([
            jnp.tile(q_norm_g[l], n_heads), jnp.tile(k_norm_g[l], n_heads),
            gmlp_norm_g[l].reshape(-1)]).astype(F32)
        proj, (w_out_b, w_ff1_b, w_ff2_b) = _in_proj(
            x2, mod3, mix_norm_g[l].reshape(1, d), w_in[l], sec_gain, seq,
            side_weights=(w_out[l], w_ff1[l], w_ff2[l]))
        proj3 = proj.reshape(b, seq, -1)
        bound, gap = _score_bound(q_norm_g[l], k_norm_g[l], rel_bias[l])
        shifted = gap * LOG2_E < MAX_SHIFT_GAP_LOG2
        bias_row = _bias_row(rel_bias[l]) - jnp.where(shifted, bound, 0.0)
        attn, gm = _mixers(proj3, bias_row, shifted.astype(jnp.int32).reshape(1),
                           w_spatial[l], b_spatial[l])
        x1, h2 = _out_proj(
            attn.reshape(b * seq, attn_w), gm.reshape(b * seq, gmlp_w), x2, mod3,
            attn_out_g[l].reshape(1, attn_w), gmlp_out_g[l].reshape(1, gmlp_w),
            ff_norm_g[l].reshape(1, d), w_out_b, seq)
        x2 = _ffn(h2, w_ff1_b, w_ff2_b, x1, mod3, seq)
    return x2.reshape(b, seq, d)
```

```python
import functools

import jax
import jax.numpy as jnp
from jax import lax
from jax.experimental import pallas as pl
from jax.experimental.pallas import tpu as pltpu

CHUNK = 64
LEFT_CHUNKS = 8
HEAD_DIM = 128
MAX_REL = 128
SPATIAL = 128
N_MOD = 6
EPS = 1e-6
NEG_INF = -1e30
LOG2_E = 1.4426950408889634

Q_BLOCK = 2 * CHUNK
KEY_WINDOW = LEFT_CHUNKS * CHUNK + Q_BLOCK
LEFT_KEYS = LEFT_CHUNKS * CHUNK
ROLL_WIDTH = 1024
SCORE_LOOKAHEAD = 2
MAX_SHIFT_GAP_LOG2 = 100.0

VMEM_LIMIT_BYTES = 56 * 1024 * 1024

ADA_ROWS = 256
IN_PROJ_ROWS = 512
W_IN_CHUNK_ROWS = 128
MIXER_GROUPS_PER_STEP = 4
OUT_PROJ_ROWS = 512
FFN_ROWS = 1024
FFN_COLS = 1024

BF16 = jnp.bfloat16
F32 = jnp.float32


def _compiler_params(n_grid_axes):
    return pltpu.CompilerParams(
        dimension_semantics=("arbitrary",) * n_grid_axes, vmem_limit_bytes=VMEM_LIMIT_BYTES)


def _rms_scale(t):
    return lax.rsqrt(jnp.mean(t * t, axis=-1, keepdims=True) + EPS)


def _gelu(t):
    return 0.5 * t * (1.0 + lax.erf(t * (0.5 ** 0.5)))


def _group_norm(t, gain, group):
    parts = []
    for s in range(0, t.shape[-1], group):
        p = t[:, s:s + group]
        parts.append(p * _rms_scale(p) * gain[:, s:s + group])
    return jnp.concatenate(parts, axis=-1)


def _ada_kernel(c_ref, w_ref, b_ref, o_ref):
    k = pl.program_id(0)
    c = c_ref[...]
    cond = (c * jax.nn.sigmoid(c)).astype(BF16)
    part = jnp.dot(cond, w_ref[...].astype(BF16), preferred_element_type=F32)

    @pl.when(k == 0)
    def _():
        o_ref[...] = part + b_ref[...]

    @pl.when(k > 0)
    def _():
        o_ref[...] += part


def _ada_mod(c, w_ada, b_ada, tk=ADA_ROWS):
    b, d = c.shape
    n = w_ada.shape[1]
    return pl.pallas_call(
        _ada_kernel,
        grid=(d // tk,),
        in_specs=[
            pl.BlockSpec((b, tk), lambda k: (0, k)),
            pl.BlockSpec((tk, n), lambda k: (k, 0)),
            pl.BlockSpec((1, n), lambda k: (0, 0)),
        ],
        out_specs=pl.BlockSpec((b, n), lambda k: (0, 0)),
        out_shape=jax.ShapeDtypeStruct((b, n), F32),
        compiler_params=_compiler_params(1),
        name="ada_mod",
    )(c, w_ada, b_ada.reshape(1, n))


def _in_proj_kernel(x_ref, mod_ref, g_ref, w_hbm, sg_ref, *rest, sec, n_side, w_chunk):
    side_in, o_ref, side_out = rest[:n_side], rest[n_side], rest[n_side + 1:2 * n_side + 1]
    w_ref, stage_ref, stage_sem = rest[2 * n_side + 1:]

    @pl.when(pl.program_id(0) == 0)
    def _():
        n_chunks = w_ref.shape[0] // w_chunk

        def chunk_copy(c):
            return pltpu.make_async_copy(
                w_hbm.at[pl.ds(c * w_chunk, w_chunk), :], stage_ref.at[c % 2], stage_sem.at[c % 2])

        chunk_copy(0).start()
        for c in range(n_chunks):
            if c + 1 < n_chunks:
                chunk_copy(c + 1).start()
            chunk_copy(c).wait()
            w_ref[c * w_chunk:(c + 1) * w_chunk, :] = stage_ref[c % 2].astype(BF16)

    for src, dst in zip(side_in, side_out):
        dst[...] = src[...].astype(BF16)

    x = x_ref[...]
    shift = mod_ref[0:1, :]
    scale = mod_ref[1:2, :]
    h = x * _rms_scale(x) * g_ref[...]
    h = (h * (1.0 + scale) + shift).astype(BF16)

    def section(s):
        return jnp.dot(h, w_ref[:, s * sec:(s + 1) * sec], preferred_element_type=F32)

    def put(s, val):
        o_ref[:, s * sec:(s + 1) * sec] = val.astype(BF16)

    put(4, _group_norm(_gelu(section(4)), sg_ref[2:3, :], HEAD_DIM))
    put(3, _gelu(section(3)))
    put(0, _group_norm(section(0), sg_ref[0:1, :], HEAD_DIM))
    put(1, _group_norm(section(1), sg_ref[1:2, :], HEAD_DIM))
    put(2, section(2))


def _in_proj(x2, mod3, g, w_in, sec_gain, seq, side_weights, tm=IN_PROJ_ROWS,
             w_chunk=W_IN_CHUNK_ROWS):
    t, d = x2.shape
    n = w_in.shape[1]
    sec = sec_gain.shape[-1]
    steps = t // tm
    steps_per_batch = seq // tm
    side_specs = [pl.BlockSpec((w.shape[0] // steps, w.shape[1]), lambda i: (i, 0))
                  for w in side_weights]
    outs = pl.pallas_call(
        functools.partial(_in_proj_kernel, sec=sec, n_side=len(side_weights), w_chunk=w_chunk),
        grid=(steps,),
        in_specs=[
            pl.BlockSpec((tm, d), lambda i: (i, 0)),
            pl.BlockSpec((None, N_MOD, d), lambda i: (i // steps_per_batch, 0, 0)),
            pl.BlockSpec((1, d), lambda i: (0, 0)),
            pl.BlockSpec(memory_space=pl.ANY),
            pl.BlockSpec(sec_gain.shape, lambda i: (0, 0)),
        ] + side_specs,
        out_specs=[pl.BlockSpec((tm, n), lambda i: (i, 0))] + side_specs,
        out_shape=[jax.ShapeDtypeStruct((t, n), BF16)]
        + [jax.ShapeDtypeStruct(w.shape, BF16) for w in side_weights],
        scratch_shapes=[pltpu.VMEM((d, n), BF16), pltpu.VMEM((2, w_chunk, n), F32),
                        pltpu.SemaphoreType.DMA((2,))],
        compiler_params=_compiler_params(1),
        name="in_proj",
    )(x2, mod3, g, w_in, sec_gain, *side_weights)
    return outs[0], outs[1:]


def _mixer_kernel(shifted_ref, q_ref, k_ref, v_ref, base_ref, u_ref, z_ref, ws_ref, bs_ref,
                  o_ref, gm_ref, bias_ref, vext_ref, kt_ref, *, seq):
    @pl.when(pl.program_id(1) == 0)
    def _():
        for g in range(base_ref.shape[0]):
            _bias_table(base_ref.at[g], bias_ref.at[g])

    def body(exact_max):
        for g in range(base_ref.shape[0]):
            lanes = lambda ref: ref.at[:, g * HEAD_DIM:(g + 1) * HEAD_DIM]
            _attention(lanes(q_ref), lanes(k_ref), lanes(v_ref), lanes(o_ref),
                       bias_ref.at[g], vext_ref.at[g], kt_ref.at[g], seq, exact_max)
        for g in range(base_ref.shape[0]):
            lanes = lambda ref: ref.at[:, g * HEAD_DIM:(g + 1) * HEAD_DIM]
            _spatial_gate(lanes(u_ref), lanes(z_ref), ws_ref.at[g], bs_ref.at[g], lanes(gm_ref),
                          seq)

    @pl.when(shifted_ref[0] == 1)
    def _():
        body(exact_max=False)

    @pl.when(shifted_ref[0] != 1)
    def _():
        body(exact_max=True)


def _spatial_gate(u_ref, z_ref, w_ref, b_ref, o_ref, seq):
    t = lax.broadcasted_iota(jnp.int32, (SPATIAL, SPATIAL), 0)
    s = lax.broadcasted_iota(jnp.int32, (SPATIAL, SPATIAL), 1)
    w = jnp.where((t // CHUNK) >= (s // CHUNK), w_ref[...], 0.0).astype(BF16)
    bias = b_ref[...]
    for n in range(seq // SPATIAL):
        rows = slice(n * SPATIAL, (n + 1) * SPATIAL)
        y = jnp.dot(w, z_ref[rows, :], preferred_element_type=F32) + bias
        o_ref[rows, :] = (u_ref[rows, :].astype(F32) * y).astype(BF16)


def _bias_table(base_ref, bias_ref):
    tab = pltpu.roll(jnp.broadcast_to(base_ref[...], (Q_BLOCK, ROLL_WIDTH)), 0, 1,
                     stride=1, stride_axis=0)[:, :KEY_WINDOW]
    r = lax.broadcasted_iota(jnp.int32, (Q_BLOCK, KEY_WINDOW), 0)
    col = lax.broadcasted_iota(jnp.int32, (Q_BLOCK, KEY_WINDOW), 1)
    qc = r // CHUNK
    kc = col // CHUNK - LEFT_CHUNKS
    bias_ref[...] = jnp.where((kc <= qc) & (kc >= qc - LEFT_CHUNKS), tab * LOG2_E, NEG_INF)


def _attention(q_ref, k_ref, v_ref, o_ref, bias_ref, vext_ref, kt_ref, seq, exact_max):
    scale = HEAD_DIM ** -0.5 * LOG2_E
    vext_ref[:, :HEAD_DIM] = v_ref[...]
    vext_ref[:, HEAD_DIM:] = jnp.ones((seq, HEAD_DIM), BF16)
    kt_ref[...] = k_ref[...].T

    def window(qb):
        p0 = qb * Q_BLOCK
        ws = max(0, p0 - LEFT_KEYS)
        return p0, ws, ws - (p0 - LEFT_KEYS)

    def scores(qb):
        p0, ws, co = window(qb)
        s = jnp.dot(q_ref[p0:p0 + Q_BLOCK, :], kt_ref[:, ws:p0 + Q_BLOCK],
                    preferred_element_type=F32)
        return s * scale + bias_ref[:, co:KEY_WINDOW]

    def finish(qb, s):
        p0, ws, _ = window(qb)
        if exact_max:
            p = jnp.exp2(s - jnp.max(s, axis=-1, keepdims=True)).astype(BF16)
        else:
            p = jnp.exp2(s).astype(BF16)
        o = jnp.dot(p, vext_ref[ws:p0 + Q_BLOCK, :], preferred_element_type=F32)
        o_ref[p0:p0 + Q_BLOCK, :] = (o[:, :HEAD_DIM] / o[:, HEAD_DIM:]).astype(BF16)

    n_blocks = seq // Q_BLOCK
    pending = [scores(qb) for qb in range(SCORE_LOOKAHEAD)]
    for qb in range(n_blocks):
        if qb + SCORE_LOOKAHEAD < n_blocks:
            pending.append(scores(qb + SCORE_LOOKAHEAD))
        finish(qb, pending.pop(0))


def _bias_row(rel_bias):
    rb = rel_bias.astype(F32)
    h = rb.shape[0]
    far = rb[:, 2 * MAX_REL:]
    row = jnp.concatenate([
        jnp.broadcast_to(far, (h, LEFT_KEYS - MAX_REL + 1)),
        rb[:, 1:2 * MAX_REL][:, ::-1],
        jnp.broadcast_to(rb[:, :1], (h, KEY_WINDOW - LEFT_KEYS - MAX_REL)),
        jnp.broadcast_to(far, (h, ROLL_WIDTH - KEY_WINDOW)),
    ], axis=1)
    return row.reshape(h, 1, ROLL_WIDTH)


def _score_bound(q_gain, k_gain, rel_bias):
    dot_bound = (1.01 * HEAD_DIM ** 0.5) * jnp.max(jnp.abs(q_gain)) * jnp.max(jnp.abs(k_gain))
    b_max = jnp.max(rel_bias)
    return dot_bound + b_max, 2.0 * dot_bound + (b_max - jnp.min(rel_bias))


def _mixers(proj3, bias_row, shifted, w_spatial, b_spatial, groups_per_step=MIXER_GROUPS_PER_STEP):
    b, seq, _ = proj3.shape
    n = w_spatial.shape[0]
    gps = groups_per_step
    assert n % gps == 0
    width = gps * HEAD_DIM
    n_steps = n // gps
    blk = lambda sec: pl.BlockSpec((None, seq, width), lambda h, bi: (bi, 0, sec * n_steps + h))
    out_blk = pl.BlockSpec((None, seq, width), lambda h, bi: (bi, 0, h))
    out = jax.ShapeDtypeStruct((b, seq, n * HEAD_DIM), BF16)
    return pl.pallas_call(
        functools.partial(_mixer_kernel, seq=seq),
        grid=(n_steps, b),
        in_specs=[
            pl.BlockSpec(memory_space=pltpu.SMEM),
            blk(0), blk(1), blk(2),
            pl.BlockSpec((gps, 1, ROLL_WIDTH), lambda h, bi: (h, 0, 0)),
            blk(3), blk(4),
            pl.BlockSpec((gps, SPATIAL, SPATIAL), lambda h, bi: (h, 0, 0)),
            pl.BlockSpec((gps, SPATIAL, 1), lambda h, bi: (h, 0, 0)),
        ],
        out_specs=[out_blk, out_blk],
        out_shape=[out, out],
        scratch_shapes=[pltpu.VMEM((gps, Q_BLOCK, KEY_WINDOW), F32),
                        pltpu.VMEM((gps, seq, 2 * HEAD_DIM), BF16),
                        pltpu.VMEM((gps, HEAD_DIM, seq), BF16)],
        compiler_params=_compiler_params(2),
        name="mixers",
    )(shifted, proj3, proj3, proj3, bias_row, proj3, proj3, w_spatial,
      b_spatial.reshape(n, SPATIAL, 1))


def _out_proj_kernel(a_ref, m_ref, x_ref, mod_ref, ga_ref, gg_ref, gf_ref, w_ref, *rest, n_side):
    side_in, (x1_ref, h2_ref), side_out = rest[:n_side], rest[n_side:n_side + 2], rest[n_side + 2:]
    for src, dst in zip(side_in, side_out):
        dst[...] = src[...].astype(BF16)

    a = a_ref[...].astype(F32)
    m = m_ref[...].astype(F32)
    mix = jnp.concatenate(
        [(a * _rms_scale(a) * ga_ref[...]).astype(BF16),
         (m * _rms_scale(m) * gg_ref[...]).astype(BF16)], axis=-1)
    y = jnp.dot(mix, w_ref[...], preferred_element_type=F32)
    gate_m = mod_ref[2:3, :]
    shift_f = mod_ref[3:4, :]
    scale_f = mod_ref[4:5, :]
    x1 = x_ref[...] + gate_m * y
    x1_ref[...] = x1
    h2 = x1 * _rms_scale(x1) * gf_ref[...]
    h2_ref[...] = (h2 * (1.0 + scale_f) + shift_f).astype(BF16)


def _out_proj(attn2, gm2, x2, mod3, ga, gg, gf, w_out, seq, side_weights, tm=OUT_PROJ_ROWS):
    t, d = x2.shape
    wa = attn2.shape[1]
    wg = gm2.shape[1]
    steps = t // tm
    steps_per_batch = seq // tm
    side_specs = [pl.BlockSpec((w.shape[0] // steps, w.shape[1]), lambda i: (i, 0))
                  for w in side_weights]
    outs = pl.pallas_call(
        functools.partial(_out_proj_kernel, n_side=len(side_weights)),
        grid=(steps,),
        in_specs=[
            pl.BlockSpec((tm, wa), lambda i: (i, 0)),
            pl.BlockSpec((tm, wg), lambda i: (i, 0)),
            pl.BlockSpec((tm, d), lambda i: (i, 0)),
            pl.BlockSpec((None, N_MOD, d), lambda i: (i // steps_per_batch, 0, 0)),
            pl.BlockSpec((1, wa), lambda i: (0, 0)),
            pl.BlockSpec((1, wg), lambda i: (0, 0)),
            pl.BlockSpec((1, d), lambda i: (0, 0)),
            pl.BlockSpec((wa + wg, d), lambda i: (0, 0)),
        ] + side_specs,
        out_specs=[
            pl.BlockSpec((tm, d), lambda i: (i, 0)),
            pl.BlockSpec((tm, d), lambda i: (i, 0)),
        ] + side_specs,
        out_shape=[jax.ShapeDtypeStruct((t, d), F32), jax.ShapeDtypeStruct((t, d), BF16)]
        + [jax.ShapeDtypeStruct(w.shape, BF16) for w in side_weights],
        compiler_params=_compiler_params(1),
        name="out_proj",
    )(attn2, gm2, x2, mod3, ga, gg, gf, w_out, *side_weights)
    return outs[0], outs[1], outs[2:]


def _ffn_kernel(h_ref, w1_ref, w2_ref, x1_hbm, mod_ref, o_ref, x1_buf, x1_sem):
    i = pl.program_id(0)
    j = pl.program_id(1)
    last = pl.num_programs(1) - 1
    tm = o_ref.shape[0]

    x1_copy = pltpu.make_async_copy(x1_hbm.at[pl.ds(i * tm, tm), :], x1_buf, x1_sem)

    def partial_out():
        a = jnp.dot(h_ref[...], w1_ref[...], preferred_element_type=F32)
        a = jnp.square(jnp.maximum(a, 0.0)).astype(BF16)
        return jnp.dot(a, w2_ref[...], preferred_element_type=F32)

    @pl.when(j == 0)
    def _():
        x1_copy.start()
        o_ref[...] = partial_out()

    @pl.when((j > 0) & (j < last))
    def _():
        o_ref[...] += partial_out()

    @pl.when(j == last)
    def _():
        gate_f = mod_ref[5:6, :]
        x1_copy.wait()
        o_ref[...] = x1_buf[...] + gate_f * (o_ref[...] + partial_out())


def _ffn(h2, w1, w2, x1, mod3, seq, tm=FFN_ROWS, tf=FFN_COLS):
    t, d = h2.shape
    f = w1.shape[1]
    steps_per_batch = seq // tm
    assert f // tf >= 2, "first and last d_ff tiles must be distinct steps"
    return pl.pallas_call(
        _ffn_kernel,
        grid=(t // tm, f // tf),
        in_specs=[
            pl.BlockSpec((tm, d), lambda i, j: (i, 0)),
            pl.BlockSpec((d, tf), lambda i, j: (0, j)),
            pl.BlockSpec((tf, d), lambda i, j: (j, 0)),
            pl.BlockSpec(memory_space=pl.ANY),
            pl.BlockSpec((None, N_MOD, d), lambda i, j: (i // steps_per_batch, 0, 0)),
        ],
        out_specs=pl.BlockSpec((tm, d), lambda i, j: (i, 0)),
        out_shape=jax.ShapeDtypeStruct((t, d), F32),
        scratch_shapes=[pltpu.VMEM((tm, d), F32), pltpu.SemaphoreType.DMA(())],
        compiler_params=_compiler_params(2),
        name="ffn",
    )(h2, w1, w2, x1, mod3)


def kernel(x, c, w_ada, b_ada, mix_norm_g, w_in, q_norm_g, k_norm_g, rel_bias, gmlp_norm_g,
           w_spatial, b_spatial, attn_out_g, gmlp_out_g, w_out, ff_norm_g, w_ff1, w_ff2):
    b, seq, d = x.shape
    depth = w_ada.shape[0]
    n_heads = rel_bias.shape[1]
    n_groups = w_spatial.shape[1]
    attn_w = n_heads * HEAD_DIM
    gmlp_w = n_groups * HEAD_DIM
    assert n_heads == n_groups and w_in.shape[2] == 3 * attn_w + 2 * gmlp_w

    x2 = x.reshape(b * seq, d)
    for l in range(depth):
        mod3 = _ada_mod(c, w_ada[l], b_ada[l]).reshape(b, N_MOD, d)
        sec_gain = jnp.stack([
            jnp.tile(q_norm_g[l], n_heads), jnp.tile(k_norm_g[l], n_heads),
            gmlp_norm_g[l].reshape(-1)]).astype(F32)
        proj, (w_out_b,) = _in_proj(
            x2, mod3, mix_norm_g[l].reshape(1, d), w_in[l], sec_gain, seq,
            side_weights=(w_out[l],))
        proj3 = proj.reshape(b, seq, -1)
        bound, gap = _score_bound(q_norm_g[l], k_norm_g[l], rel_bias[l])
        shifted = gap * LOG2_E < MAX_SHIFT_GAP_LOG2
        bias_row = _bias_row(rel_bias[l]) - jnp.where(shifted, bound, 0.0)
        attn, gm = _mixers(proj3, bias_row, shifted.astype(jnp.int32).reshape(1),
                           w_spatial[l], b_spatial[l])
        x1, h2, (w_ff1_b, w_ff2_b) = _out_proj(
            attn.reshape(b * seq, attn_w), gm.reshape(b * seq, gmlp_w), x2, mod3,
            attn_out_g[l].reshape(1, attn_w), gmlp_out_g[l].reshape(1, gmlp_w),
            ff_norm_g[l].reshape(1, d), w_out_b, seq, side_weights=(w_ff1[l], w_ff2[l]))
        x2 = _ffn(h2, w_ff1_b, w_ff2_b, x1, mod3, seq)
    return x2.reshape(b, seq, d)
```

```python
import functools

import jax
import jax.numpy as jnp
from jax import lax
from jax.experimental import pallas as pl
from jax.experimental.pallas import tpu as pltpu

CHUNK = 64
LEFT_CHUNKS = 8
HEAD_DIM = 128
MAX_REL = 128
SPATIAL = 128
N_MOD = 6
EPS = 1e-6
NEG_INF = -1e30
LOG2_E = 1.4426950408889634

Q_BLOCK = 2 * CHUNK
KEY_WINDOW = LEFT_CHUNKS * CHUNK + Q_BLOCK
LEFT_KEYS = LEFT_CHUNKS * CHUNK
ROLL_WIDTH = 1024
SCORE_LOOKAHEAD = 2
MAX_SHIFT_GAP_LOG2 = 100.0

VMEM_LIMIT_BYTES = 56 * 1024 * 1024

ADA_ROWS = 256
IN_PROJ_ROWS = 512
W_IN_CHUNK_ROWS = 32
MIXER_GROUPS_PER_STEP = 4
OUT_PROJ_ROWS = 512
FFN_ROWS = 1024
FFN_COLS = 1024

BF16 = jnp.bfloat16
F32 = jnp.float32


def _compiler_params(n_grid_axes):
    return pltpu.CompilerParams(
        dimension_semantics=("arbitrary",) * n_grid_axes, vmem_limit_bytes=VMEM_LIMIT_BYTES)


def _rms_scale(t):
    return lax.rsqrt(jnp.mean(t * t, axis=-1, keepdims=True) + EPS)


def _gelu(t):
    return 0.5 * t * (1.0 + lax.erf(t * (0.5 ** 0.5)))


def _group_norm(t, gain, group):
    parts = []
    for s in range(0, t.shape[-1], group):
        p = t[:, s:s + group]
        parts.append(p * _rms_scale(p) * gain[:, s:s + group])
    return jnp.concatenate(parts, axis=-1)


def _ada_kernel(c_ref, w_ref, b_ref, o_ref):
    k = pl.program_id(0)
    c = c_ref[...]
    cond = (c * jax.nn.sigmoid(c)).astype(BF16)
    part = jnp.dot(cond, w_ref[...].astype(BF16), preferred_element_type=F32)

    @pl.when(k == 0)
    def _():
        o_ref[...] = part + b_ref[...]

    @pl.when(k > 0)
    def _():
        o_ref[...] += part


def _ada_mod(c, w_ada, b_ada, tk=ADA_ROWS):
    b, d = c.shape
    n = w_ada.shape[1]
    return pl.pallas_call(
        _ada_kernel,
        grid=(d // tk,),
        in_specs=[
            pl.BlockSpec((b, tk), lambda k: (0, k)),
            pl.BlockSpec((tk, n), lambda k: (k, 0)),
            pl.BlockSpec((1, n), lambda k: (0, 0)),
        ],
        out_specs=pl.BlockSpec((b, n), lambda k: (0, 0)),
        out_shape=jax.ShapeDtypeStruct((b, n), F32),
        compiler_params=_compiler_params(1),
        name="ada_mod",
    )(c, w_ada, b_ada.reshape(1, n))


def _in_proj_kernel(x_ref, mod_ref, g_ref, w_hbm, sg_ref, *rest, sec, n_side, w_chunk):
    side_in, o_ref, side_out = rest[:n_side], rest[n_side], rest[n_side + 1:2 * n_side + 1]
    w_ref, stage_ref, stage_sem = rest[2 * n_side + 1:]

    @pl.when(pl.program_id(0) == 0)
    def _():
        n_chunks = w_ref.shape[0] // w_chunk

        def chunk_copy(c):
            return pltpu.make_async_copy(
                w_hbm.at[pl.ds(c * w_chunk, w_chunk), :], stage_ref.at[c % 2], stage_sem.at[c % 2])

        chunk_copy(0).start()
        for c in range(n_chunks):
            if c + 1 < n_chunks:
                chunk_copy(c + 1).start()
            chunk_copy(c).wait()
            w_ref[c * w_chunk:(c + 1) * w_chunk, :] = stage_ref[c % 2].astype(BF16)

    for src, dst in zip(side_in, side_out):
        dst[...] = src[...].astype(BF16)

    x = x_ref[...]
    shift = mod_ref[0:1, :]
    scale = mod_ref[1:2, :]
    h = x * _rms_scale(x) * g_ref[...]
    h = (h * (1.0 + scale) + shift).astype(BF16)

    def section(s):
        return jnp.dot(h, w_ref[:, s * sec:(s + 1) * sec], preferred_element_type=F32)

    def put(s, val):
        o_ref[:, s * sec:(s + 1) * sec] = val.astype(BF16)

    put(4, _group_norm(_gelu(section(4)), sg_ref[2:3, :], HEAD_DIM))
    put(3, _gelu(section(3)))
    put(0, _group_norm(section(0), sg_ref[0:1, :], HEAD_DIM))
    put(1, _group_norm(section(1), sg_ref[1:2, :], HEAD_DIM))
    put(2, section(2))


def _in_proj(x2, mod3, g, w_in, sec_gain, seq, side_weights, tm=IN_PROJ_ROWS,
             w_chunk=W_IN_CHUNK_ROWS):
    t, d = x2.shape
    n = w_in.shape[1]
    sec = sec_gain.shape[-1]
    steps = t // tm
    steps_per_batch = seq // tm
    side_specs = [pl.BlockSpec((w.shape[0] // steps, w.shape[1]), lambda i: (i, 0))
                  for w in side_weights]
    outs = pl.pallas_call(
        functools.partial(_in_proj_kernel, sec=sec, n_side=len(side_weights), w_chunk=w_chunk),
        grid=(steps,),
        in_specs=[
            pl.BlockSpec((tm, d), lambda i: (i, 0)),
            pl.BlockSpec((None, N_MOD, d), lambda i: (i // steps_per_batch, 0, 0)),
            pl.BlockSpec((1, d), lambda i: (0, 0)),
            pl.BlockSpec(memory_space=pl.ANY),
            pl.BlockSpec(sec_gain.shape, lambda i: (0, 0)),
        ] + side_specs,
        out_specs=[pl.BlockSpec((tm, n), lambda i: (i, 0))] + side_specs,
        out_shape=[jax.ShapeDtypeStruct((t, n), BF16)]
        + [jax.ShapeDtypeStruct(w.shape, BF16) for w in side_weights],
        scratch_shapes=[pltpu.VMEM((d, n), BF16), pltpu.VMEM((2, w_chunk, n), F32),
                        pltpu.SemaphoreType.DMA((2,))],
        compiler_params=_compiler_params(1),
        name="in_proj",
    )(x2, mod3, g, w_in, sec_gain, *side_weights)
    return outs[0], outs[1:]


def _mixer_kernel(shifted_ref, q_ref, k_ref, v_ref, base_ref, u_ref, z_ref, ws_ref, bs_ref,
                  o_ref, gm_ref, bias_ref, vext_ref, kt_ref, *, seq):
    @pl.when(pl.program_id(1) == 0)
    def _():
        for g in range(base_ref.shape[0]):
            _bias_table(base_ref.at[g], bias_ref.at[g])

    def body(exact_max):
        for g in range(base_ref.shape[0]):
            lanes = lambda ref: ref.at[:, g * HEAD_DIM:(g + 1) * HEAD_DIM]
            _attention(lanes(q_ref), lanes(k_ref), lanes(v_ref), lanes(o_ref),
                       bias_ref.at[g], vext_ref.at[g], kt_ref.at[g], seq, exact_max)
        for g in range(base_ref.shape[0]):
            lanes = lambda ref: ref.at[:, g * HEAD_DIM:(g + 1) * HEAD_DIM]
            _spatial_gate(lanes(u_ref), lanes(z_ref), ws_ref.at[g], bs_ref.at[g], lanes(gm_ref),
                          seq)

    @pl.when(shifted_ref[0] == 1)
    def _():
        body(exact_max=False)

    @pl.when(shifted_ref[0] != 1)
    def _():
        body(exact_max=True)


def _spatial_gate(u_ref, z_ref, w_ref, b_ref, o_ref, seq):
    t = lax.broadcasted_iota(jnp.int32, (SPATIAL, SPATIAL), 0)
    s = lax.broadcasted_iota(jnp.int32, (SPATIAL, SPATIAL), 1)
    w = jnp.where((t // CHUNK) >= (s // CHUNK), w_ref[...], 0.0).astype(BF16)
    bias = b_ref[...]
    for n in range(seq // SPATIAL):
        rows = slice(n * SPATIAL, (n + 1) * SPATIAL)
        y = jnp.dot(w, z_ref[rows, :], preferred_element_type=F32) + bias
        o_ref[rows, :] = (u_ref[rows, :].astype(F32) * y).astype(BF16)


def _bias_table(base_ref, bias_ref):
    tab = pltpu.roll(jnp.broadcast_to(base_ref[...], (Q_BLOCK, ROLL_WIDTH)), 0, 1,
                     stride=1, stride_axis=0)[:, :KEY_WINDOW]
    r = lax.broadcasted_iota(jnp.int32, (Q_BLOCK, KEY_WINDOW), 0)
    col = lax.broadcasted_iota(jnp.int32, (Q_BLOCK, KEY_WINDOW), 1)
    qc = r // CHUNK
    kc = col // CHUNK - LEFT_CHUNKS
    bias_ref[...] = jnp.where((kc <= qc) & (kc >= qc - LEFT_CHUNKS), tab * LOG2_E, NEG_INF)


def _attention(q_ref, k_ref, v_ref, o_ref, bias_ref, vext_ref, kt_ref, seq, exact_max):
    scale = HEAD_DIM ** -0.5 * LOG2_E
    vext_ref[:, :HEAD_DIM] = v_ref[...]
    vext_ref[:, HEAD_DIM:] = jnp.ones((seq, HEAD_DIM), BF16)
    kt_ref[...] = k_ref[...].T

    def window(qb):
        p0 = qb * Q_BLOCK
        ws = max(0, p0 - LEFT_KEYS)
        return p0, ws, ws - (p0 - LEFT_KEYS)

    def scores(qb):
        p0, ws, co = window(qb)
        s = jnp.dot(q_ref[p0:p0 + Q_BLOCK, :], kt_ref[:, ws:p0 + Q_BLOCK],
                    preferred_element_type=F32)
        return s * scale + bias_ref[:, co:KEY_WINDOW]

    def finish(qb, s):
        p0, ws, _ = window(qb)
        if exact_max:
            p = jnp.exp2(s - jnp.max(s, axis=-1, keepdims=True)).astype(BF16)
        else:
            p = jnp.exp2(s).astype(BF16)
        o = jnp.dot(p, vext_ref[ws:p0 + Q_BLOCK, :], preferred_element_type=F32)
        o_ref[p0:p0 + Q_BLOCK, :] = (o[:, :HEAD_DIM] / o[:, HEAD_DIM:]).astype(BF16)

    n_blocks = seq // Q_BLOCK
    pending = [scores(qb) for qb in range(SCORE_LOOKAHEAD)]
    for qb in range(n_blocks):
        if qb + SCORE_LOOKAHEAD < n_blocks:
            pending.append(scores(qb + SCORE_LOOKAHEAD))
        finish(qb, pending.pop(0))


def _bias_row(rel_bias):
    rb = rel_bias.astype(F32)
    h = rb.shape[0]
    far = rb[:, 2 * MAX_REL:]
    row = jnp.concatenate([
        jnp.broadcast_to(far, (h, LEFT_KEYS - MAX_REL + 1)),
        rb[:, 1:2 * MAX_REL][:, ::-1],
        jnp.broadcast_to(rb[:, :1], (h, KEY_WINDOW - LEFT_KEYS - MAX_REL)),
        jnp.broadcast_to(far, (h, ROLL_WIDTH - KEY_WINDOW)),
    ], axis=1)
    return row.reshape(h, 1, ROLL_WIDTH)


def _score_bound(q_gain, k_gain, rel_bias):
    dot_bound = (1.01 * HEAD_DIM ** 0.5) * jnp.max(jnp.abs(q_gain)) * jnp.max(jnp.abs(k_gain))
    b_max = jnp.max(rel_bias)
    return dot_bound + b_max, 2.0 * dot_bound + (b_max - jnp.min(rel_bias))


def _mixers(proj3, bias_row, shifted, w_spatial, b_spatial, groups_per_step=MIXER_GROUPS_PER_STEP):
    b, seq, _ = proj3.shape
    n = w_spatial.shape[0]
    gps = groups_per_step
    assert n % gps == 0
    width = gps * HEAD_DIM
    n_steps = n // gps
    blk = lambda sec: pl.BlockSpec((None, seq, width), lambda h, bi: (bi, 0, sec * n_steps + h))
    out_blk = pl.BlockSpec((None, seq, width), lambda h, bi: (bi, 0, h))
    out = jax.ShapeDtypeStruct((b, seq, n * HEAD_DIM), BF16)
    return pl.pallas_call(
        functools.partial(_mixer_kernel, seq=seq),
        grid=(n_steps, b),
        in_specs=[
            pl.BlockSpec(memory_space=pltpu.SMEM),
            blk(0), blk(1), blk(2),
            pl.BlockSpec((gps, 1, ROLL_WIDTH), lambda h, bi: (h, 0, 0)),
            blk(3), blk(4),
            pl.BlockSpec((gps, SPATIAL, SPATIAL), lambda h, bi: (h, 0, 0)),
            pl.BlockSpec((gps, SPATIAL, 1), lambda h, bi: (h, 0, 0)),
        ],
        out_specs=[out_blk, out_blk],
        out_shape=[out, out],
        scratch_shapes=[pltpu.VMEM((gps, Q_BLOCK, KEY_WINDOW), F32),
                        pltpu.VMEM((gps, seq, 2 * HEAD_DIM), BF16),
                        pltpu.VMEM((gps, HEAD_DIM, seq), BF16)],
        compiler_params=_compiler_params(2),
        name="mixers",
    )(shifted, proj3, proj3, proj3, bias_row, proj3, proj3, w_spatial,
      b_spatial.reshape(n, SPATIAL, 1))


def _out_proj_kernel(a_ref, m_ref, x_ref, mod_ref, ga_ref, gg_ref, gf_ref, w_ref, x1_ref, h2_ref):
    a = a_ref[...].astype(F32)
    m = m_ref[...].astype(F32)
    mix = jnp.concatenate(
        [(a * _rms_scale(a) * ga_ref[...]).astype(BF16),
         (m * _rms_scale(m) * gg_ref[...]).astype(BF16)], axis=-1)
    y = jnp.dot(mix, w_ref[...], preferred_element_type=F32)
    gate_m = mod_ref[2:3, :]
    shift_f = mod_ref[3:4, :]
    scale_f = mod_ref[4:5, :]
    x1 = x_ref[...] + gate_m * y
    x1_ref[...] = x1
    h2 = x1 * _rms_scale(x1) * gf_ref[...]
    h2_ref[...] = (h2 * (1.0 + scale_f) + shift_f).astype(BF16)


def _out_proj(attn2, gm2, x2, mod3, ga, gg, gf, w_out, seq, tm=OUT_PROJ_ROWS):
    t, d = x2.shape
    wa = attn2.shape[1]
    wg = gm2.shape[1]
    steps_per_batch = seq // tm
    return pl.pallas_call(
        _out_proj_kernel,
        grid=(t // tm,),
        in_specs=[
            pl.BlockSpec((tm, wa), lambda i: (i, 0)),
            pl.BlockSpec((tm, wg), lambda i: (i, 0)),
            pl.BlockSpec((tm, d), lambda i: (i, 0)),
            pl.BlockSpec((None, N_MOD, d), lambda i: (i // steps_per_batch, 0, 0)),
            pl.BlockSpec((1, wa), lambda i: (0, 0)),
            pl.BlockSpec((1, wg), lambda i: (0, 0)),
            pl.BlockSpec((1, d), lambda i: (0, 0)),
            pl.BlockSpec((wa + wg, d), lambda i: (0, 0)),
        ],
        out_specs=[
            pl.BlockSpec((tm, d), lambda i: (i, 0)),
            pl.BlockSpec((tm, d), lambda i: (i, 0)),
        ],
        out_shape=[jax.ShapeDtypeStruct((t, d), F32), jax.ShapeDtypeStruct((t, d), BF16)],
        compiler_params=_compiler_params(1),
        name="out_proj",
    )(attn2, gm2, x2, mod3, ga, gg, gf, w_out)


def _ffn_kernel(h_ref, w1_ref, w2_ref, x1_hbm, mod_ref, o_ref, x1_buf, x1_sem):
    i = pl.program_id(0)
    j = pl.program_id(1)
    last = pl.num_programs(1) - 1
    tm = o_ref.shape[0]

    x1_copy = pltpu.make_async_copy(x1_hbm.at[pl.ds(i * tm, tm), :], x1_buf, x1_sem)

    def partial_out():
        a = jnp.dot(h_ref[...], w1_ref[...], preferred_element_type=F32)
        a = jnp.square(jnp.maximum(a, 0.0)).astype(BF16)
        return jnp.dot(a, w2_ref[...], preferred_element_type=F32)

    @pl.when(j == 0)
    def _():
        x1_copy.start()
        o_ref[...] = partial_out()

    @pl.when((j > 0) & (j < last))
    def _():
        o_ref[...] += partial_out()

    @pl.when(j == last)
    def _():
        gate_f = mod_ref[5:6, :]
        x1_copy.wait()
        o_ref[...] = x1_buf[...] + gate_f * (o_ref[...] + partial_out())


def _ffn(h2, w1, w2, x1, mod3, seq, tm=FFN_ROWS, tf=FFN_COLS):
    t, d = h2.shape
    f = w1.shape[1]
    steps_per_batch = seq // tm
    assert f // tf >= 2, "first and last d_ff tiles must be distinct steps"
    return pl.pallas_call(
        _ffn_kernel,
        grid=(t // tm, f // tf),
        in_specs=[
            pl.BlockSpec((tm, d), lambda i, j: (i, 0)),
            pl.BlockSpec((d, tf), lambda i, j: (0, j)),
            pl.BlockSpec((tf, d), lambda i, j: (j, 0)),
            pl.BlockSpec(memory_space=pl.ANY),
            pl.BlockSpec((None, N_MOD, d), lambda i, j: (i // steps_per_batch, 0, 0)),
        ],
        out_specs=pl.BlockSpec((tm, d), lambda i, j: (i, 0)),
        out_shape=jax.ShapeDtypeStruct((t, d), F32),
        scratch_shapes=[pltpu.VMEM((tm, d), F32), pltpu.SemaphoreType.DMA(())],
        compiler_params=_compiler_params(2),
        name="ffn",
    )(h2, w1, w2, x1, mod3)


def kernel(x, c, w_ada, b_ada, mix_norm_g, w_in, q_norm_g, k_norm_g, rel_bias, gmlp_norm_g,
           w_spatial, b_spatial, attn_out_g, gmlp_out_g, w_out, ff_norm_g, w_ff1, w_ff2):
    b, seq, d = x.shape
    depth = w_ada.shape[0]
    n_heads = rel_bias.shape[1]
    n_groups = w_spatial.shape[1]
    attn_w = n_heads * HEAD_DIM
    gmlp_w = n_groups * HEAD_DIM
    assert n_heads == n_groups and w_in.shape[2] == 3 * attn_w + 2 * gmlp_w

    x2 = x.reshape(b * seq, d)
    for l in range(depth):
        mod3 = _ada_mod(c, w_ada[l], b_ada[l]).reshape(b, N_MOD, d)
        sec_gain = jnp.stack([
            jnp.tile(q_norm_g[l], n_heads), jnp.tile(k_norm_g[l], n_heads),
            gmlp_norm_g[l].reshape(-1)]).astype(F32)
        proj, (w_out_b, w_ff1_b, w_ff2_b) = _in_proj(
            x2, mod3, mix_norm_g[l].reshape(1, d), w_in[l], sec_gain, seq,
            side_weights=(w_out[l], w_ff1[l], w_ff2[l]))
        proj3 = proj.reshape(b, seq, -1)
        bound, gap = _score_bound(q_norm_g[l], k_norm_g[l], rel_bias[l])
        shifted = gap * LOG2_E < MAX_SHIFT_GAP_LOG2
        bias_row = _bias_row(rel_bias[l]) - jnp.where(shifted, bound, 0.0)
        attn, gm = _mixers(proj3, bias_row, shifted.astype(jnp.int32).reshape(1),
                           w_spatial[l], b_spatial[l])
        x1, h2 = _out_proj(
            attn.reshape(b * seq, attn_w), gm.reshape(b * seq, gmlp_w), x2, mod3,
            attn_out_g[l].reshape(1, attn_w), gmlp_out_g[l].reshape(1, gmlp_w),
            ff_norm_g[l].reshape(1, d), w_out_b, seq)
        x2 = _ffn(h2, w_ff1_b, w_ff2_b, x1, mod3, seq)
    return x2.reshape(b, seq, d)
```

```python
import functools

import jax
import jax.numpy as jnp
from jax import lax
from jax.experimental import pallas as pl
from jax.experimental.pallas import tpu as pltpu

CHUNK = 64
LEFT_CHUNKS = 8
HEAD_DIM = 128
MAX_REL = 128
SPATIAL = 128
N_MOD = 6
EPS = 1e-6
NEG_INF = -1e30
LOG2_E = 1.4426950408889634

Q_BLOCK = 2 * CHUNK
KEY_WINDOW = LEFT_CHUNKS * CHUNK + Q_BLOCK
LEFT_KEYS = LEFT_CHUNKS * CHUNK
ROLL_WIDTH = 1024
SCORE_LOOKAHEAD = 2
MAX_SHIFT_GAP_LOG2 = 100.0

VMEM_LIMIT_BYTES = 56 * 1024 * 1024

ADA_ROWS = 256
IN_PROJ_ROWS = 256
W_IN_CHUNK_ROWS = 128
MIXER_GROUPS_PER_STEP = 4
OUT_PROJ_ROWS = 512
FFN_ROWS = 1024
FFN_COLS = 1024

BF16 = jnp.bfloat16
F32 = jnp.float32


def _compiler_params(n_grid_axes):
    return pltpu.CompilerParams(
        dimension_semantics=("arbitrary",) * n_grid_axes, vmem_limit_bytes=VMEM_LIMIT_BYTES)


def _rms_scale(t):
    return lax.rsqrt(jnp.mean(t * t, axis=-1, keepdims=True) + EPS)


def _gelu(t):
    return 0.5 * t * (1.0 + lax.erf(t * (0.5 ** 0.5)))


def _group_norm(t, gain, group):
    parts = []
    for s in range(0, t.shape[-1], group):
        p = t[:, s:s + group]
        parts.append(p * _rms_scale(p) * gain[:, s:s + group])
    return jnp.concatenate(parts, axis=-1)


def _ada_kernel(c_ref, w_ref, b_ref, o_ref):
    k = pl.program_id(0)
    c = c_ref[...]
    cond = (c * jax.nn.sigmoid(c)).astype(BF16)
    part = jnp.dot(cond, w_ref[...].astype(BF16), preferred_element_type=F32)

    @pl.when(k == 0)
    def _():
        o_ref[...] = part + b_ref[...]

    @pl.when(k > 0)
    def _():
        o_ref[...] += part


def _ada_mod(c, w_ada, b_ada, tk=ADA_ROWS):
    b, d = c.shape
    n = w_ada.shape[1]
    return pl.pallas_call(
        _ada_kernel,
        grid=(d // tk,),
        in_specs=[
            pl.BlockSpec((b, tk), lambda k: (0, k)),
            pl.BlockSpec((tk, n), lambda k: (k, 0)),
            pl.BlockSpec((1, n), lambda k: (0, 0)),
        ],
        out_specs=pl.BlockSpec((b, n), lambda k: (0, 0)),
        out_shape=jax.ShapeDtypeStruct((b, n), F32),
        compiler_params=_compiler_params(1),
        name="ada_mod",
    )(c, w_ada, b_ada.reshape(1, n))


def _in_proj_kernel(x_ref, mod_ref, g_ref, w_hbm, sg_ref, rg_ref, *rest, sec, n_side, w_chunk):
    side_in, o_ref, side_out = rest[:n_side], rest[n_side], rest[n_side + 1:2 * n_side + 1]
    w_ref, stage_ref, stage_sem = rest[2 * n_side + 1:]

    @pl.when(pl.program_id(0) == 0)
    def _():
        n_chunks = w_ref.shape[0] // w_chunk

        def chunk_copy(c):
            return pltpu.make_async_copy(
                w_hbm.at[pl.ds(c * w_chunk, w_chunk), :], stage_ref.at[c % 2], stage_sem.at[c % 2])

        chunk_copy(0).start()
        for c in range(n_chunks):
            if c + 1 < n_chunks:
                chunk_copy(c + 1).start()
            chunk_copy(c).wait()
            w_ref[c * w_chunk:(c + 1) * w_chunk, :] = stage_ref[c % 2].astype(BF16)

    for n, (src, dst) in enumerate(zip(side_in, side_out)):
        w = src[...] * rg_ref[...] if n == 0 else src[...]
        dst[...] = w.astype(BF16)

    x = x_ref[...]
    shift = mod_ref[0:1, :]
    scale = mod_ref[1:2, :]
    h = x * _rms_scale(x) * g_ref[...]
    h = (h * (1.0 + scale) + shift).astype(BF16)

    def section(s):
        return jnp.dot(h, w_ref[:, s * sec:(s + 1) * sec], preferred_element_type=F32)

    def put(s, val):
        o_ref[:, s * sec:(s + 1) * sec] = val.astype(BF16)

    put(4, _group_norm(_gelu(section(4)), sg_ref[2:3, :], HEAD_DIM))
    put(3, _gelu(section(3)))
    put(0, _group_norm(section(0), sg_ref[0:1, :], HEAD_DIM))
    put(1, _group_norm(section(1), sg_ref[1:2, :], HEAD_DIM))
    put(2, section(2))


def _in_proj(x2, mod3, g, w_in, sec_gain, seq, side_weights, row_gain, tm=IN_PROJ_ROWS,
             w_chunk=W_IN_CHUNK_ROWS):
    t, d = x2.shape
    n = w_in.shape[1]
    sec = sec_gain.shape[-1]
    steps = t // tm
    steps_per_batch = seq // tm
    side_specs = [pl.BlockSpec((w.shape[0] // steps, w.shape[1]), lambda i: (i, 0))
                  for w in side_weights]
    outs = pl.pallas_call(
        functools.partial(_in_proj_kernel, sec=sec, n_side=len(side_weights), w_chunk=w_chunk),
        grid=(steps,),
        in_specs=[
            pl.BlockSpec((tm, d), lambda i: (i, 0)),
            pl.BlockSpec((None, N_MOD, d), lambda i: (i // steps_per_batch, 0, 0)),
            pl.BlockSpec((1, d), lambda i: (0, 0)),
            pl.BlockSpec(memory_space=pl.ANY),
            pl.BlockSpec(sec_gain.shape, lambda i: (0, 0)),
            pl.BlockSpec((row_gain.shape[0] // steps, 1), lambda i: (i, 0)),
        ] + side_specs,
        out_specs=[pl.BlockSpec((tm, n), lambda i: (i, 0))] + side_specs,
        out_shape=[jax.ShapeDtypeStruct((t, n), BF16)]
        + [jax.ShapeDtypeStruct(w.shape, BF16) for w in side_weights],
        scratch_shapes=[pltpu.VMEM((d, n), BF16), pltpu.VMEM((2, w_chunk, n), F32),
                        pltpu.SemaphoreType.DMA((2,))],
        compiler_params=_compiler_params(1),
        name="in_proj",
    )(x2, mod3, g, w_in, sec_gain, row_gain, *side_weights)
    return outs[0], outs[1:]


def _mixer_kernel(shifted_ref, q_ref, k_ref, v_ref, base_ref, u_ref, z_ref, ws_ref, bs_ref,
                  o_ref, gm_ref, bias_ref, vext_ref, kt_ref, *, seq):
    @pl.when(pl.program_id(1) == 0)
    def _():
        for g in range(base_ref.shape[0]):
            _bias_table(base_ref.at[g], bias_ref.at[g])

    def body(exact_max):
        for g in range(base_ref.shape[0]):
            lanes = lambda ref: ref.at[:, g * HEAD_DIM:(g + 1) * HEAD_DIM]
            _attention(lanes(q_ref), lanes(k_ref), lanes(v_ref), lanes(o_ref),
                       bias_ref.at[g], vext_ref.at[g], kt_ref.at[g], seq, exact_max)
        for g in range(base_ref.shape[0]):
            lanes = lambda ref: ref.at[:, g * HEAD_DIM:(g + 1) * HEAD_DIM]
            _spatial_gate(lanes(u_ref), lanes(z_ref), ws_ref.at[g], bs_ref.at[g], lanes(gm_ref),
                          seq)

    @pl.when(shifted_ref[0] == 1)
    def _():
        body(exact_max=False)

    @pl.when(shifted_ref[0] != 1)
    def _():
        body(exact_max=True)


def _spatial_gate(u_ref, z_ref, w_ref, b_ref, o_ref, seq):
    t = lax.broadcasted_iota(jnp.int32, (SPATIAL, SPATIAL), 0)
    s = lax.broadcasted_iota(jnp.int32, (SPATIAL, SPATIAL), 1)
    w = jnp.where((t // CHUNK) >= (s // CHUNK), w_ref[...], 0.0).astype(BF16)
    bias = b_ref[...]
    for n in range(seq // SPATIAL):
        rows = slice(n * SPATIAL, (n + 1) * SPATIAL)
        y = jnp.dot(w, z_ref[rows, :], preferred_element_type=F32) + bias
        o_ref[rows, :] = (u_ref[rows, :].astype(F32) * y).astype(BF16)


def _bias_table(base_ref, bias_ref):
    tab = pltpu.roll(jnp.broadcast_to(base_ref[...], (Q_BLOCK, ROLL_WIDTH)), 0, 1,
                     stride=1, stride_axis=0)[:, :KEY_WINDOW]
    r = lax.broadcasted_iota(jnp.int32, (Q_BLOCK, KEY_WINDOW), 0)
    col = lax.broadcasted_iota(jnp.int32, (Q_BLOCK, KEY_WINDOW), 1)
    qc = r // CHUNK
    kc = col // CHUNK - LEFT_CHUNKS
    bias_ref[...] = jnp.where((kc <= qc) & (kc >= qc - LEFT_CHUNKS), tab * LOG2_E, NEG_INF)


def _attention(q_ref, k_ref, v_ref, o_ref, bias_ref, vext_ref, kt_ref, seq, exact_max):
    scale = HEAD_DIM ** -0.5 * LOG2_E
    vext_ref[:, :HEAD_DIM] = v_ref[...]
    vext_ref[:, HEAD_DIM:] = jnp.ones((seq, HEAD_DIM), BF16)
    kt_ref[...] = k_ref[...].T

    def window(qb):
        p0 = qb * Q_BLOCK
        ws = max(0, p0 - LEFT_KEYS)
        return p0, ws, ws - (p0 - LEFT_KEYS)

    def scores(qb):
        p0, ws, co = window(qb)
        s = jnp.dot(q_ref[p0:p0 + Q_BLOCK, :], kt_ref[:, ws:p0 + Q_BLOCK],
                    preferred_element_type=F32)
        return s * scale + bias_ref[:, co:KEY_WINDOW]

    def finish(qb, s):
        p0, ws, _ = window(qb)
        if exact_max:
            p = jnp.exp2(s - jnp.max(s, axis=-1, keepdims=True)).astype(BF16)
        else:
            p = jnp.exp2(s).astype(BF16)
        o = jnp.dot(p, vext_ref[ws:p0 + Q_BLOCK, :], preferred_element_type=F32)
        o_ref[p0:p0 + Q_BLOCK, :] = (o[:, :HEAD_DIM] / o[:, HEAD_DIM:]).astype(BF16)

    n_blocks = seq // Q_BLOCK
    pending = [scores(qb) for qb in range(SCORE_LOOKAHEAD)]
    for qb in range(n_blocks):
        if qb + SCORE_LOOKAHEAD < n_blocks:
            pending.append(scores(qb + SCORE_LOOKAHEAD))
        finish(qb, pending.pop(0))


def _bias_row(rel_bias):
    rb = rel_bias.astype(F32)
    h = rb.shape[0]
    far = rb[:, 2 * MAX_REL:]
    row = jnp.concatenate([
        jnp.broadcast_to(far, (h, LEFT_KEYS - MAX_REL + 1)),
        rb[:, 1:2 * MAX_REL][:, ::-1],
        jnp.broadcast_to(rb[:, :1], (h, KEY_WINDOW - LEFT_KEYS - MAX_REL)),
        jnp.broadcast_to(far, (h, ROLL_WIDTH - KEY_WINDOW)),
    ], axis=1)
    return row.reshape(h, 1, ROLL_WIDTH)


def _score_bound(q_gain, k_gain, rel_bias):
    dot_bound = (1.01 * HEAD_DIM ** 0.5) * jnp.max(jnp.abs(q_gain)) * jnp.max(jnp.abs(k_gain))
    b_max = jnp.max(rel_bias)
    return dot_bound + b_max, 2.0 * dot_bound + (b_max - jnp.min(rel_bias))


def _mixers(proj3, bias_row, shifted, w_spatial, b_spatial, groups_per_step=MIXER_GROUPS_PER_STEP):
    b, seq, _ = proj3.shape
    n = w_spatial.shape[0]
    gps = groups_per_step
    assert n % gps == 0
    width = gps * HEAD_DIM
    n_steps = n // gps
    blk = lambda sec: pl.BlockSpec((None, seq, width), lambda h, bi: (bi, 0, sec * n_steps + h))
    out_blk = pl.BlockSpec((None, seq, width), lambda h, bi: (bi, 0, h))
    out = jax.ShapeDtypeStruct((b, seq, n * HEAD_DIM), BF16)
    return pl.pallas_call(
        functools.partial(_mixer_kernel, seq=seq),
        grid=(n_steps, b),
        in_specs=[
            pl.BlockSpec(memory_space=pltpu.SMEM),
            blk(0), blk(1), blk(2),
            pl.BlockSpec((gps, 1, ROLL_WIDTH), lambda h, bi: (h, 0, 0)),
            blk(3), blk(4),
            pl.BlockSpec((gps, SPATIAL, SPATIAL), lambda h, bi: (h, 0, 0)),
            pl.BlockSpec((gps, SPATIAL, 1), lambda h, bi: (h, 0, 0)),
        ],
        out_specs=[out_blk, out_blk],
        out_shape=[out, out],
        scratch_shapes=[pltpu.VMEM((gps, Q_BLOCK, KEY_WINDOW), F32),
                        pltpu.VMEM((gps, seq, 2 * HEAD_DIM), BF16),
                        pltpu.VMEM((gps, HEAD_DIM, seq), BF16)],
        compiler_params=_compiler_params(2),
        name="mixers",
    )(shifted, proj3, proj3, proj3, bias_row, proj3, proj3, w_spatial,
      b_spatial.reshape(n, SPATIAL, 1))


def _out_proj_kernel(a_ref, m_ref, x_ref, mod_ref, gf_ref, w_ref, x1_ref, h2_ref):
    a = a_ref[...]
    m = m_ref[...]
    wa = a.shape[1]
    ya = jnp.dot(a, w_ref[:wa, :], preferred_element_type=F32)
    ym = jnp.dot(m, w_ref[wa:, :], preferred_element_type=F32)
    y = _rms_scale(a.astype(F32)) * ya + _rms_scale(m.astype(F32)) * ym
    gate_m = mod_ref[2:3, :]
    shift_f = mod_ref[3:4, :]
    scale_f = mod_ref[4:5, :]
    x1 = x_ref[...] + gate_m * y
    x1_ref[...] = x1
    h2 = x1 * _rms_scale(x1) * gf_ref[...]
    h2_ref[...] = (h2 * (1.0 + scale_f) + shift_f).astype(BF16)


def _out_proj(attn2, gm2, x2, mod3, gf, w_out, seq, tm=OUT_PROJ_ROWS):
    t, d = x2.shape
    wa = attn2.shape[1]
    wg = gm2.shape[1]
    steps_per_batch = seq // tm
    return pl.pallas_call(
        _out_proj_kernel,
        grid=(t // tm,),
        in_specs=[
            pl.BlockSpec((tm, wa), lambda i: (i, 0)),
            pl.BlockSpec((tm, wg), lambda i: (i, 0)),
            pl.BlockSpec((tm, d), lambda i: (i, 0)),
            pl.BlockSpec((None, N_MOD, d), lambda i: (i // steps_per_batch, 0, 0)),
            pl.BlockSpec((1, d), lambda i: (0, 0)),
            pl.BlockSpec((wa + wg, d), lambda i: (0, 0)),
        ],
        out_specs=[
            pl.BlockSpec((tm, d), lambda i: (i, 0)),
            pl.BlockSpec((tm, d), lambda i: (i, 0)),
        ],
        out_shape=[jax.ShapeDtypeStruct((t, d), F32), jax.ShapeDtypeStruct((t, d), BF16)],
        compiler_params=_compiler_params(1),
        name="out_proj",
    )(attn2, gm2, x2, mod3, gf, w_out)


def _ffn_kernel(h_ref, w1_ref, w2_ref, x1_hbm, mod_ref, o_ref, x1_buf, x1_sem):
    i = pl.program_id(0)
    j = pl.program_id(1)
    last = pl.num_programs(1) - 1
    tm = o_ref.shape[0]

    x1_copy = pltpu.make_async_copy(x1_hbm.at[pl.ds(i * tm, tm), :], x1_buf, x1_sem)

    def partial_out():
        a = jnp.dot(h_ref[...], w1_ref[...], preferred_element_type=F32)
        a = jnp.square(jnp.maximum(a, 0.0)).astype(BF16)
        return jnp.dot(a, w2_ref[...], preferred_element_type=F32)

    @pl.when(j == 0)
    def _():
        x1_copy.start()
        o_ref[...] = partial_out()

    @pl.when((j > 0) & (j < last))
    def _():
        o_ref[...] += partial_out()

    @pl.when(j == last)
    def _():
        gate_f = mod_ref[5:6, :]
        x1_copy.wait()
        o_ref[...] = x1_buf[...] + gate_f * (o_ref[...] + partial_out())


def _ffn(h2, w1, w2, x1, mod3, seq, tm=FFN_ROWS, tf=FFN_COLS):
    t, d = h2.shape
    f = w1.shape[1]
    steps_per_batch = seq // tm
    assert f // tf >= 2, "first and last d_ff tiles must be distinct steps"
    return pl.pallas_call(
        _ffn_kernel,
        grid=(t // tm, f // tf),
        in_specs=[
            pl.BlockSpec((tm, d), lambda i, j: (i, 0)),
            pl.BlockSpec((d, tf), lambda i, j: (0, j)),
            pl.BlockSpec((tf, d), lambda i, j: (j, 0)),
            pl.BlockSpec(memory_space=pl.ANY),
            pl.BlockSpec((None, N_MOD, d), lambda i, j: (i // steps_per_batch, 0, 0)),
        ],
        out_specs=pl.BlockSpec((tm, d), lambda i, j: (i, 0)),
        out_shape=jax.ShapeDtypeStruct((t, d), F32),
        scratch_shapes=[pltpu.VMEM((tm, d), F32), pltpu.SemaphoreType.DMA(())],
        compiler_params=_compiler_params(2),
        name="ffn",
    )(h2, w1, w2, x1, mod3)


def kernel(x, c, w_ada, b_ada, mix_norm_g, w_in, q_norm_g, k_norm_g, rel_bias, gmlp_norm_g,
           w_spatial, b_spatial, attn_out_g, gmlp_out_g, w_out, ff_norm_g, w_ff1, w_ff2):
    b, seq, d = x.shape
    depth = w_ada.shape[0]
    n_heads = rel_bias.shape[1]
    n_groups = w_spatial.shape[1]
    attn_w = n_heads * HEAD_DIM
    gmlp_w = n_groups * HEAD_DIM
    assert n_heads == n_groups and w_in.shape[2] == 3 * attn_w + 2 * gmlp_w

    x2 = x.reshape(b * seq, d)
    for l in range(depth):
        mod3 = _ada_mod(c, w_ada[l], b_ada[l]).reshape(b, N_MOD, d)
        sec_gain = jnp.stack([
            jnp.tile(q_norm_g[l], n_heads), jnp.tile(k_norm_g[l], n_heads),
            gmlp_norm_g[l].reshape(-1)]).astype(F32)
        proj, (w_out_b, w_ff1_b, w_ff2_b) = _in_proj(
            x2, mod3, mix_norm_g[l].reshape(1, d), w_in[l], sec_gain, seq,
            side_weights=(w_out[l], w_ff1[l], w_ff2[l]),
            row_gain=jnp.concatenate([attn_out_g[l], gmlp_out_g[l]]).astype(F32).reshape(-1, 1))
        proj3 = proj.reshape(b, seq, -1)
        bound, gap = _score_bound(q_norm_g[l], k_norm_g[l], rel_bias[l])
        shifted = gap * LOG2_E < MAX_SHIFT_GAP_LOG2
        bias_row = _bias_row(rel_bias[l]) - jnp.where(shifted, bound, 0.0)
        attn, gm = _mixers(proj3, bias_row, shifted.astype(jnp.int32).reshape(1),
                           w_spatial[l], b_spatial[l])
        x1, h2 = _out_proj(
            attn.reshape(b * seq, attn_w), gm.reshape(b * seq, gmlp_w), x2, mod3,
            ff_norm_g[l].reshape(1, d), w_out_b, seq)
        x2 = _ffn(h2, w_ff1_b, w_ff2_b, x1, mod3, seq)
    return x2.reshape(b, seq, d)
```
